```python
import math
import jax, jax.numpy as jnp
from jax import lax
import numpy as np

D_MODEL = 1024
BATCH = 32
SEQ = 2048
DEPTH = 2

EXPAND = 2
BRANCH_W = EXPAND * D_MODEL
N_A = DEPTH // 2
N_B = DEPTH - N_A
CHUNK = 128
GROUP_W = 128
N_GROUPS = BRANCH_W // GROUP_W
DIFF_HEAD_DIM = 128
N_DIFF_HEADS = BRANCH_W // (2 * DIFF_HEAD_DIM)
ATT_W = N_DIFF_HEADS * 2 * DIFF_HEAD_DIM
Q_BLOCK = 128
EPS = 1e-6

kernel_name = "yoco_gmlp_diffattn_hybrid"


def rmsnorm(x, g):
    x32 = x.astype(jnp.float32)
    y = x32 * lax.rsqrt(jnp.mean(x32 * x32, axis=-1, keepdims=True) + EPS)
    return (y * g.astype(jnp.float32)).astype(x.dtype)


def layernorm(x, g, b):
    x32 = x.astype(jnp.float32)
    mu = jnp.mean(x32, axis=-1, keepdims=True)
    xc = x32 - mu
    y = xc * lax.rsqrt(jnp.mean(xc * xc, axis=-1, keepdims=True) + EPS)
    return (y * g.astype(jnp.float32) + b.astype(jnp.float32)).astype(x.dtype)


def lambda_init_fn(layer_idx):
    return 0.8 - 0.6 * math.exp(-0.3 * layer_idx)


def mixer_a(hn, w_in, ln_g, ln_b, w_s, b_s, w_out):
    bsz, seq, _ = hn.shape
    u, v, z = jnp.split(hn @ w_in, 3, axis=-1)
    u = jax.nn.gelu(u)
    v = layernorm(jax.nn.gelu(v), ln_g, ln_b)
    vc = v.reshape(bsz, seq // CHUNK, CHUNK, N_GROUPS, GROUP_W)
    causal = jnp.tril(jnp.ones((CHUNK, CHUNK), dtype=bool))
    ws = jnp.where(causal[None], w_s, jnp.zeros((), w_s.dtype))
    sv = jnp.einsum('gts,bnsgc->bntgc', ws, vc) + jnp.transpose(b_s)[None, None, :, :, None]
    y = u * sv.reshape(bsz, seq, BRANCH_W) * jax.nn.silu(z)
    return y @ w_out


def shared_kv(h, kv_norm_g, w_kv):
    bsz, seq, _ = h.shape
    k, v = jnp.split(rmsnorm(h, kv_norm_g) @ w_kv, 2, axis=-1)
    k = k.reshape(bsz, seq, N_DIFF_HEADS, 2, DIFF_HEAD_DIM)
    v = v.reshape(bsz, seq, N_DIFF_HEADS, 2 * DIFF_HEAD_DIM)
    return k, v


def diff_attention(q, k, v, lam):
    seq = q.shape[1]
    scale = DIFF_HEAD_DIM ** -0.5
    outs = []
    for i in range(seq // Q_BLOCK):
        start, end = i * Q_BLOCK, (i + 1) * Q_BLOCK
        qs = q[:, start:end]
        ks = k[:, :end]
        vs = v[:, :end]
        s = jnp.einsum('bqhnd,bkhnd->bhnqk', qs, ks).astype(jnp.float32) * scale
        q_pos = jnp.arange(start, end)
        k_pos = jnp.arange(end)
        mask = k_pos[None, :] <= q_pos[:, None]
        s = jnp.where(mask, s, -jnp.inf)
        p = jax.nn.softmax(s, axis=-1)
        w = p[:, :, 0] - lam * p[:, :, 1]
        outs.append(jnp.einsum('bhqk,bkhe->bqhe', w.astype(vs.dtype), vs))
    return jnp.concatenate(outs, axis=1)


def mixer_b(hn, k, v, w_qz, lq1, lk1, lq2, lk2, subln_g, w_o, lam_init):
    bsz, seq, _ = hn.shape
    q, z = jnp.split(hn @ w_qz, 2, axis=-1)
    q = q.reshape(bsz, seq, N_DIFF_HEADS, 2, DIFF_HEAD_DIM)
    f32 = jnp.float32
    lam = (jnp.exp(jnp.sum(lq1.astype(f32) * lk1.astype(f32)))
           - jnp.exp(jnp.sum(lq2.astype(f32) * lk2.astype(f32))) + lam_init)
    o = diff_attention(q, k, v, lam)
    o = rmsnorm(o, subln_g) * (1.0 - lam_init)
    y = o.reshape(bsz, seq, ATT_W) * jax.nn.silu(z)
    return y @ w_o


def setup_inputs(seed: int = 0) -> dict:
    key = jax.random.key(seed)
    ks = jax.random.split(key, 24)
    f32 = jnp.float32
    D, E, d = D_MODEL, BRANCH_W, DIFF_HEAD_DIM
    nrm = lambda k, shape, s: jax.random.normal(k, shape, f32) * s
    return {
        "x": jax.random.normal(ks[0], (BATCH, SEQ, D), f32),
        "a_norm_g": 1.0 + nrm(ks[1], (N_A, D), 0.02),
        "a_w_in": nrm(ks[2], (N_A, D, 3 * E), D ** -0.5),
        "a_ln_g": 1.0 + nrm(ks[3], (N_A, E), 0.02),
        "a_ln_b": nrm(ks[4], (N_A, E), 0.02),
        "a_w_s": nrm(ks[5], (N_A, N_GROUPS, CHUNK, CHUNK), CHUNK ** -0.5),
        "a_b_s": 1.0 + nrm(ks[6], (N_A, N_GROUPS, CHUNK), 0.02),
        "a_w_out": nrm(ks[7], (N_A, E, D), E ** -0.5),
        "b_norm_g": 1.0 + nrm(ks[8], (N_B, D), 0.02),
        "b_w_qz": nrm(ks[9], (N_B, D, 2 * ATT_W), D ** -0.5),
        "b_lam_q1": nrm(ks[10], (N_B, d), 0.1),
        "b_lam_k1": nrm(ks[11], (N_B, d), 0.1),
        "b_lam_q2": nrm(ks[12], (N_B, d), 0.1),
        "b_lam_k2": nrm(ks[13], (N_B, d), 0.1),
        "b_subln_g": 1.0 + nrm(ks[14], (N_B, 2 * d), 0.02),
        "b_w_o": nrm(ks[15], (N_B, ATT_W, D), ATT_W ** -0.5),
        "kv_norm_g": 1.0 + nrm(ks[16], (D,), 0.02),
        "w_kv": nrm(ks[17], (D, 2 * ATT_W), D ** -0.5),
        "final_g": 1.0 + nrm(ks[18], (D,), 0.02),
    }


def reference(x, a_norm_g, a_w_in, a_ln_g, a_ln_b, a_w_s, a_b_s, a_w_out,
              b_norm_g, b_w_qz, b_lam_q1, b_lam_k1, b_lam_q2, b_lam_k2, b_subln_g, b_w_o,
              kv_norm_g, w_kv, final_g):
    h = x
    k_sh, v_sh = None, None
    for l in range(DEPTH):
        if l < N_A:
            h = h + mixer_a(rmsnorm(h, a_norm_g[l]), a_w_in[l], a_ln_g[l], a_ln_b[l],
                            a_w_s[l], a_b_s[l], a_w_out[l])
        else:
            if l == N_A:
                k_sh, v_sh = shared_kv(h, kv_norm_g, w_kv)
            j = l - N_A
            h = h + mixer_b(rmsnorm(h, b_norm_g[j]), k_sh, v_sh, b_w_qz[j],
                            b_lam_q1[j], b_lam_k1[j], b_lam_q2[j], b_lam_k2[j],
                            b_subln_g[j], b_w_o[j], lambda_init_fn(l))
    return rmsnorm(h, final_g)
```

```python
import functools
import math

import jax
import jax.numpy as jnp
from jax import lax
from jax.experimental import pallas as pl
from jax.experimental.pallas import tpu as pltpu

F32 = jnp.float32
BF16 = jnp.bfloat16

CHUNK = 128
GROUP_W = 128
DIFF_HEAD_DIM = 128
EPS = 1e-6

TOKEN_TILE = 256
GATE_COLS = 512
Q_TILE = 256
VMEM_LIMIT_BYTES = 56 * 1024 * 1024


def _sigmoid(x):
    return 1.0 / (1.0 + jnp.exp(-x))


def _resident(shape):
    return pl.BlockSpec(shape, lambda *_: (0,) * len(shape), pipeline_mode=pl.Buffered(1))


def _layer_a_kernel(x_ref, ng_ref, win_ref, lng_ref, lnb_ref, ws_ref, bs_ref, wout_ref,
                    o_ref, wsm_s, bsb_s, vn_s, y_s):
    tm, _ = x_ref.shape
    e = vn_s.shape[1]
    n_groups = ws_ref.shape[0]

    @pl.when(pl.program_id(0) == 0)
    def _():
        row = lax.broadcasted_iota(jnp.int32, (CHUNK, CHUNK), 0)
        col = lax.broadcasted_iota(jnp.int32, (CHUNK, CHUNK), 1)
        causal = col <= row
        for g in range(n_groups):
            wsm_s[g] = jnp.where(causal, ws_ref[g], 0.0).astype(BF16)
            bsb_s[g] = jnp.broadcast_to(bs_ref[g], (CHUNK, GROUP_W))

    x = x_ref[...]
    hn = x * lax.rsqrt(jnp.mean(x * x, axis=-1, keepdims=True) + EPS)
    hn = (hn * ng_ref[...]).astype(BF16)

    v = jax.nn.gelu(jnp.dot(hn, win_ref[:, e:2 * e], preferred_element_type=F32))
    mu = jnp.mean(v, axis=-1, keepdims=True)
    vc = v - mu
    vn = vc * lax.rsqrt(jnp.mean(vc * vc, axis=-1, keepdims=True) + EPS)
    vn_s[...] = (vn * lng_ref[...] + lnb_ref[...]).astype(BF16)

    for c0 in range(0, e, GATE_COLS):
        u = jax.nn.gelu(jnp.dot(hn, win_ref[:, c0:c0 + GATE_COLS], preferred_element_type=F32))
        z = jnp.dot(hn, win_ref[:, 2 * e + c0:2 * e + c0 + GATE_COLS], preferred_element_type=F32)
        gate = z * _sigmoid(z)
        for r0 in range(0, tm, CHUNK):
            for gc in range(0, GATE_COLS, GROUP_W):
                g = (c0 + gc) // GROUP_W
                cols = slice(c0 + gc, c0 + gc + GROUP_W)
                sv = jnp.dot(wsm_s[g], vn_s[r0:r0 + CHUNK, cols], preferred_element_type=F32)
                sv = sv + bsb_s[g]
                y = u[r0:r0 + CHUNK, gc:gc + GROUP_W] * sv * gate[r0:r0 + CHUNK, gc:gc + GROUP_W]
                y_s[r0:r0 + CHUNK, cols] = y.astype(BF16)

    o_ref[...] = x + jnp.dot(y_s[...], wout_ref[...], preferred_element_type=F32)


def _layer_a(x, norm_g, w_in, ln_g, ln_b, w_s, b_s, w_out):
    t, d = x.shape
    e = w_out.shape[0]
    n_groups = w_s.shape[0]
    assert t % TOKEN_TILE == 0 and TOKEN_TILE % CHUNK == 0 and e % GATE_COLS == 0
    return pl.pallas_call(
        _layer_a_kernel,
        grid=(t // TOKEN_TILE,),
        in_specs=[
            pl.BlockSpec((TOKEN_TILE, d), lambda i: (i, 0)),
            _resident((1, d)),
            _resident((d, 3 * e)),
            _resident((1, e)),
            _resident((1, e)),
            _resident((n_groups, CHUNK, CHUNK)),
            _resident((n_groups, CHUNK, 1)),
            _resident((e, d)),
        ],
        out_specs=pl.BlockSpec((TOKEN_TILE, d), lambda i: (i, 0)),
        out_shape=jax.ShapeDtypeStruct((t, d), F32),
        scratch_shapes=[
            pltpu.VMEM((n_groups, CHUNK, CHUNK), BF16),
            pltpu.VMEM((n_groups, CHUNK, GROUP_W), F32),
            pltpu.VMEM((TOKEN_TILE, e), BF16),
            pltpu.VMEM((TOKEN_TILE, e), BF16),
        ],
        compiler_params=pltpu.CompilerParams(
            dimension_semantics=("arbitrary",), vmem_limit_bytes=VMEM_LIMIT_BYTES),
        name="layer_a",
    )(x, norm_g.reshape(1, d), w_in.astype(BF16), ln_g.reshape(1, e), ln_b.reshape(1, e),
      w_s, b_s.reshape(n_groups, CHUNK, 1), w_out.astype(BF16))


def _proj_kernel(h_ref, kvg_ref, qg_ref, wkv_ref, wqz_ref, k_ref, v_ref, q_ref, z_ref):
    w = k_ref.shape[1]
    h = h_ref[...]
    hr = h * lax.rsqrt(jnp.mean(h * h, axis=-1, keepdims=True) + EPS)
    hk = (hr * kvg_ref[...]).astype(BF16)
    hq = (hr * qg_ref[...]).astype(BF16)
    k_ref[...] = jnp.dot(hk, wkv_ref[:, :w], preferred_element_type=F32).astype(BF16)
    v_ref[...] = jnp.dot(hk, wkv_ref[:, w:], preferred_element_type=F32).astype(BF16)
    q_ref[...] = jnp.dot(hq, wqz_ref[:, :w], preferred_element_type=F32).astype(BF16)
    z_ref[...] = jnp.dot(hq, wqz_ref[:, w:], preferred_element_type=F32)


def _proj(h, kv_norm_g, q_norm_g, w_kv, w_qz):
    t, d = h.shape
    w = w_kv.shape[1] // 2
    tile = pl.BlockSpec((TOKEN_TILE, w), lambda i: (i, 0))
    return pl.pallas_call(
        _proj_kernel,
        grid=(t // TOKEN_TILE,),
        in_specs=[
            pl.BlockSpec((TOKEN_TILE, d), lambda i: (i, 0)),
            _resident((1, d)),
            _resident((1, d)),
            _resident((d, 2 * w)),
            _resident((d, 2 * w)),
        ],
        out_specs=[tile, tile, tile, tile],
        out_shape=[jax.ShapeDtypeStruct((t, w), BF16)] * 3 + [jax.ShapeDtypeStruct((t, w), F32)],
        compiler_params=pltpu.CompilerParams(
            dimension_semantics=("arbitrary",), vmem_limit_bytes=VMEM_LIMIT_BYTES),
        name="proj",
    )(h, kv_norm_g.reshape(1, d), q_norm_g.reshape(1, d), w_kv.astype(BF16), w_qz.astype(BF16))


def _attn_kernel(lq1_ref, lk1_ref, lq2_ref, lk2_ref, sg_ref, q_ref, k_ref, v_ref, z_ref, y_ref,
                 *, lam_init):
    seq = q_ref.shape[0]
    d = DIFF_HEAD_DIM
    scale = d ** -0.5
    lam = (jnp.exp(jnp.sum(lq1_ref[...] * lk1_ref[...], keepdims=True))
           - jnp.exp(jnp.sum(lq2_ref[...] * lk2_ref[...], keepdims=True)) + lam_init)

    for r0 in range(0, seq, Q_TILE):
        p_len = r0 + Q_TILE
        q_pos = r0 + lax.broadcasted_iota(jnp.int32, (Q_TILE, p_len), 0)
        k_pos = lax.broadcasted_iota(jnp.int32, (Q_TILE, p_len), 1)
        mask = k_pos <= q_pos
        probs = []
        for n in range(2):
            qn = q_ref[r0:r0 + Q_TILE, n * d:(n + 1) * d]
            kn = k_ref[0:p_len, n * d:(n + 1) * d]
            s = lax.dot_general(qn, kn, (((1,), (1,)), ((), ())), preferred_element_type=F32)
            s = jnp.where(mask, s * scale, -jnp.inf)
            ex = jnp.exp(s - jnp.max(s, axis=-1, keepdims=True))
            probs.append(ex * (1.0 / jnp.sum(ex, axis=-1, keepdims=True)))
        w = (probs[0] - lam * probs[1]).astype(BF16)
        o = jnp.dot(w, v_ref[0:p_len, :], preferred_element_type=F32)
        on = o * lax.rsqrt(jnp.mean(o * o, axis=-1, keepdims=True) + EPS)
        on = on * sg_ref[...] * (1.0 - lam_init)
        z = z_ref[r0:r0 + Q_TILE, :]
        y_ref[r0:r0 + Q_TILE, :] = (on * (z * _sigmoid(z))).astype(BF16)


def _attn(q, k, v, z, lq1, lk1, lq2, lk2, subln_g, lam_init, batch, seq):
    t, w = q.shape
    hw = 2 * DIFF_HEAD_DIM
    n_heads = w // hw
    assert seq % Q_TILE == 0 and t == batch * seq
    head = pl.BlockSpec((seq, hw), lambda b, h: (b, h))
    vec = _resident((1, DIFF_HEAD_DIM))
    return pl.pallas_call(
        functools.partial(_attn_kernel, lam_init=lam_init),
        grid=(batch, n_heads),
        in_specs=[vec, vec, vec, vec, _resident((1, hw)), head, head, head, head],
        out_specs=head,
        out_shape=jax.ShapeDtypeStruct((t, w), BF16),
        compiler_params=pltpu.CompilerParams(
            dimension_semantics=("arbitrary", "arbitrary"), vmem_limit_bytes=VMEM_LIMIT_BYTES),
        name="attn",
    )(lq1.reshape(1, -1), lk1.reshape(1, -1), lq2.reshape(1, -1), lk2.reshape(1, -1),
      subln_g.reshape(1, hw), q, k, v, z)


def _out_proj_kernel(h_ref, y_ref, wo_ref, fg_ref, o_ref):
    h = h_ref[...] + jnp.dot(y_ref[...], wo_ref[...], preferred_element_type=F32)
    hn = h * lax.rsqrt(jnp.mean(h * h, axis=-1, keepdims=True) + EPS)
    o_ref[...] = hn * fg_ref[...]


def _out_proj(h, y, w_o, final_g):
    t, d = h.shape
    w = y.shape[1]
    return pl.pallas_call(
        _out_proj_kernel,
        grid=(t // TOKEN_TILE,),
        in_specs=[
            pl.BlockSpec((TOKEN_TILE, d), lambda i: (i, 0)),
            pl.BlockSpec((TOKEN_TILE, w), lambda i: (i, 0)),
            _resident((w, d)),
            _resident((1, d)),
        ],
        out_specs=pl.BlockSpec((TOKEN_TILE, d), lambda i: (i, 0)),
        out_shape=jax.ShapeDtypeStruct((t, d), F32),
        compiler_params=pltpu.CompilerParams(
            dimension_semantics=("arbitrary",), vmem_limit_bytes=VMEM_LIMIT_BYTES),
        name="out_proj",
    )(h, y, w_o.astype(BF16), final_g.reshape(1, d))


def _lambda_init(layer_idx):
    return 0.8 - 0.6 * math.exp(-0.3 * layer_idx)


def kernel(x, a_norm_g, a_w_in, a_ln_g, a_ln_b, a_w_s, a_b_s, a_w_out,
           b_norm_g, b_w_qz, b_lam_q1, b_lam_k1, b_lam_q2, b_lam_k2, b_subln_g, b_w_o,
           kv_norm_g, w_kv, final_g):
    batch, seq, d = x.shape
    n_a, n_b = a_norm_g.shape[0], b_norm_g.shape[0]
    assert n_a == 1 and n_b == 1, "one gMLP layer followed by one attention layer"
    h = x.reshape(batch * seq, d)
    h = _layer_a(h, a_norm_g[0], a_w_in[0], a_ln_g[0], a_ln_b[0], a_w_s[0], a_b_s[0], a_w_out[0])
    k, v, q, z = _proj(h, kv_norm_g, b_norm_g[0], w_kv, b_w_qz[0])
    y = _attn(q, k, v, z, b_lam_q1[0], b_lam_k1[0], b_lam_q2[0], b_lam_k2[0], b_subln_g[0],
              _lambda_init(n_a), batch, seq)
    out = _out_proj(h, y, b_w_o[0], final_g)
    return out.reshape(batch, seq, d)
```

```python
import functools
import math

import jax
import jax.numpy as jnp
from jax import lax
from jax.experimental import pallas as pl
from jax.experimental.pallas import tpu as pltpu

F32 = jnp.float32
BF16 = jnp.bfloat16

CHUNK = 128
GROUP_W = 128
DIFF_HEAD_DIM = 128
EPS = 1e-6
SUBLANES = 8
QK_EXP2_SCALE = DIFF_HEAD_DIM ** -0.5 * math.log2(math.e)

TOKEN_TILE = 256
GATE_COLS = 512
Q_TILE = 256
KEY_CHUNK = 128
TILE_SLOTS = 2
VMEM_LIMIT_BYTES = 56 * 1024 * 1024

_NT = (((1,), (1,)), ((), ()))


def _sigmoid(x):
    return 1.0 / (1.0 + jnp.exp(-x))


def _resident(shape):
    return pl.BlockSpec(shape, lambda *_: (0,) * len(shape), pipeline_mode=pl.Buffered(1))


def _layer_a_kernel(x_ref, ng_ref, win_ref, lng_ref, lnb_ref, ws_ref, bs_ref, wout_ref,
                    o_ref, wsm_s, bsb_s, vn_s, y_s):
    tm, _ = x_ref.shape
    e = vn_s.shape[1]
    n_groups = ws_ref.shape[0]

    @pl.when(pl.program_id(0) == 0)
    def _():
        row = lax.broadcasted_iota(jnp.int32, (CHUNK, CHUNK), 0)
        col = lax.broadcasted_iota(jnp.int32, (CHUNK, CHUNK), 1)
        causal = col <= row
        for g in range(n_groups):
            wsm_s[g] = jnp.where(causal, ws_ref[g], 0.0).astype(BF16)
            bsb_s[g] = jnp.broadcast_to(bs_ref[g], (CHUNK, GROUP_W))

    x = x_ref[...]
    hn = x * lax.rsqrt(jnp.mean(x * x, axis=-1, keepdims=True) + EPS)
    hn = (hn * ng_ref[...]).astype(BF16)

    v = jax.nn.gelu(jnp.dot(hn, win_ref[:, e:2 * e], preferred_element_type=F32))
    mu = jnp.mean(v, axis=-1, keepdims=True)
    vc = v - mu
    vn = vc * lax.rsqrt(jnp.mean(vc * vc, axis=-1, keepdims=True) + EPS)
    vn_s[...] = (vn * lng_ref[...] + lnb_ref[...]).astype(BF16)

    for c0 in range(0, e, GATE_COLS):
        u = jax.nn.gelu(jnp.dot(hn, win_ref[:, c0:c0 + GATE_COLS], preferred_element_type=F32))
        z = jnp.dot(hn, win_ref[:, 2 * e + c0:2 * e + c0 + GATE_COLS], preferred_element_type=F32)
        gate = z * _sigmoid(z)
        for r0 in range(0, tm, CHUNK):
            for gc in range(0, GATE_COLS, GROUP_W):
                g = (c0 + gc) // GROUP_W
                cols = slice(c0 + gc, c0 + gc + GROUP_W)
                sv = jnp.dot(wsm_s[g], vn_s[r0:r0 + CHUNK, cols], preferred_element_type=F32)
                sv = sv + bsb_s[g]
                y = u[r0:r0 + CHUNK, gc:gc + GROUP_W] * sv * gate[r0:r0 + CHUNK, gc:gc + GROUP_W]
                y_s[r0:r0 + CHUNK, cols] = y.astype(BF16)

    o_ref[...] = x + jnp.dot(y_s[...], wout_ref[...], preferred_element_type=F32)


def _layer_a(x, norm_g, w_in, ln_g, ln_b, w_s, b_s, w_out):
    t, d = x.shape
    e = w_out.shape[0]
    n_groups = w_s.shape[0]
    assert t % TOKEN_TILE == 0 and TOKEN_TILE % CHUNK == 0 and e % GATE_COLS == 0
    return pl.pallas_call(
        _layer_a_kernel,
        grid=(t // TOKEN_TILE,),
        in_specs=[
            pl.BlockSpec((TOKEN_TILE, d), lambda i: (i, 0)),
            _resident((1, d)),
            _resident((d, 3 * e)),
            _resident((1, e)),
            _resident((1, e)),
            _resident((n_groups, CHUNK, CHUNK)),
            _resident((n_groups, CHUNK, 1)),
            _resident((e, d)),
        ],
        out_specs=pl.BlockSpec((TOKEN_TILE, d), lambda i: (i, 0)),
        out_shape=jax.ShapeDtypeStruct((t, d), F32),
        scratch_shapes=[
            pltpu.VMEM((n_groups, CHUNK, CHUNK), BF16),
            pltpu.VMEM((n_groups, CHUNK, GROUP_W), F32),
            pltpu.VMEM((TOKEN_TILE, e), BF16),
            pltpu.VMEM((TOKEN_TILE, e), BF16),
        ],
        compiler_params=pltpu.CompilerParams(
            dimension_semantics=("arbitrary",), vmem_limit_bytes=VMEM_LIMIT_BYTES),
        name="layer_a",
    )(x, norm_g.reshape(1, d), w_in.astype(BF16), ln_g.reshape(1, e), ln_b.reshape(1, e),
      w_s, b_s.reshape(n_groups, CHUNK, 1), w_out.astype(BF16))


def _proj_kernel(h_ref, kvg_ref, qg_ref, wk_ref, wq_ref, wvt_ref, wzt_ref, k_ref, q_ref, vt_ref, zt_ref):
    h = h_ref[...]
    hr = h * lax.rsqrt(jnp.mean(h * h, axis=-1, keepdims=True) + EPS)
    hk = (hr * kvg_ref[...]).astype(BF16)
    hq = (hr * qg_ref[...]).astype(BF16)
    k_ref[...] = jnp.dot(hk, wk_ref[...], preferred_element_type=F32).astype(BF16)
    q = jnp.dot(hq, wq_ref[...], preferred_element_type=F32)
    q_ref[...] = (q * QK_EXP2_SCALE).astype(BF16)
    vt_ref[...] = lax.dot_general(wvt_ref[...], hk, _NT, preferred_element_type=F32).astype(BF16)
    zt_ref[...] = lax.dot_general(wzt_ref[...], hq, _NT, preferred_element_type=F32)


def _proj(h, kv_norm_g, q_norm_g, w_kv, w_qz):
    t, d = h.shape
    w = w_kv.shape[1] // 2
    rows = pl.BlockSpec((TOKEN_TILE, w), lambda i: (i, 0))
    cols = pl.BlockSpec((w, TOKEN_TILE), lambda i: (0, i))
    return pl.pallas_call(
        _proj_kernel,
        grid=(t // TOKEN_TILE,),
        in_specs=[
            pl.BlockSpec((TOKEN_TILE, d), lambda i: (i, 0)),
            _resident((1, d)),
            _resident((1, d)),
            _resident((d, w)),
            _resident((d, w)),
            _resident((w, d)),
            _resident((w, d)),
        ],
        out_specs=[rows, rows, cols, cols],
        out_shape=[jax.ShapeDtypeStruct((t, w), BF16), jax.ShapeDtypeStruct((t, w), BF16),
                   jax.ShapeDtypeStruct((w, t), BF16), jax.ShapeDtypeStruct((w, t), F32)],
        compiler_params=pltpu.CompilerParams(
            dimension_semantics=("arbitrary",), vmem_limit_bytes=VMEM_LIMIT_BYTES),
        name="proj",
    )(h, kv_norm_g.reshape(1, d), q_norm_g.reshape(1, d),
      w_kv[:, :w].astype(BF16), w_qz[:, :w].astype(BF16),
      w_kv[:, w:].T.astype(BF16), w_qz[:, w:].T.astype(BF16))


def _attn_kernel(lq1_ref, lk1_ref, lq2_ref, lk2_ref, sg_ref, q_ref, k_ref, vt_ref, zt_ref, yt_ref,
                 ex_s, wt_s, *, lam_init):
    seq = q_ref.shape[0]
    d = DIFF_HEAD_DIM
    lam =(jnp.exp(jnp.sum(lq1_ref[...] * lk1_ref[...], keepdims=True))
           - jnp.exp(jnp.sum(lq2_ref[...] * lk2_ref[...], keepdims=True)) + lam_init)
    key_in_chunk = lax.broadcasted_iota(jnp.int32, (KEY_CHUNK, Q_TILE), 0)
    query_in_tile = lax.broadcasted_iota(jnp.int32, (KEY_CHUNK, Q_TILE), 1)
    gain = sg_ref[...] * (1.0 - lam_init)

    groups = KEY_CHUNK // SUBLANES
    grouped = (groups, SUBLANES, Q_TILE)

    def scores(r0):
        return [lax.dot_general(k_ref[0:r0 + Q_TILE, n * d:(n + 1) * d],
                                q_ref[r0:r0 + Q_TILE, n * d:(n + 1) * d],
                                _NT, preferred_element_type=F32) for n in range(2)]

    def weights(r0, st_pair):
        p_len = r0 + Q_TILE
        chunks = range(0, p_len, KEY_CHUNK)
        slot = (r0 // Q_TILE) % TILE_SLOTS
        coef = []
        for n, st in enumerate(st_pair):
            m_c, l_c = [], []
            for c0 in chunks:
                s = st[c0:c0 + KEY_CHUNK, :]
                if c0 >= r0:
                    s = jnp.where(key_in_chunk + (c0 - r0) <= query_in_tile, s, -jnp.inf)
                s = s.reshape(grouped)
                m = jnp.max(s, axis=0)
                shift = m
                if c0 >= r0:
                    shift = jnp.where(m == -jnp.inf, 0.0, m)
                ex = jnp.exp2(s - shift)
                ex_s[slot, n, c0:c0 + KEY_CHUNK, :] = ex.reshape(KEY_CHUNK, Q_TILE)
                m_c.append(m)
                l_c.append(jnp.sum(ex, axis=0))
            m_row = jnp.max(functools.reduce(jnp.maximum, m_c), axis=0, keepdims=True)
            m_row = jnp.broadcast_to(m_row, (SUBLANES, Q_TILE))
            alpha = [jnp.exp2(m - m_row) for m in m_c]
            l_row = jnp.sum(sum(a * l for a, l in zip(alpha, l_c)), axis=0, keepdims=True)
            inv = 1.0 / l_row
            if n == 1:
                inv = lam * inv
            inv = jnp.broadcast_to(inv, (SUBLANES, Q_TILE))
            coef.append([a * inv for a in alpha])
        for j, c0 in enumerate(chunks):
            w = (ex_s[slot, 0, c0:c0 + KEY_CHUNK, :].reshape(grouped) * coef[0][j]
                 - ex_s[slot, 1, c0:c0 + KEY_CHUNK, :].reshape(grouped) * coef[1][j])
            wt_s[slot, c0:c0 + KEY_CHUNK, :] = w.reshape(KEY_CHUNK, Q_TILE).astype(BF16)

    def output(r0):
        p_len = r0 + Q_TILE
        slot = (r0 // Q_TILE) % TILE_SLOTS
        ot = jnp.dot(vt_ref[:, 0:p_len], wt_s[slot, 0:p_len, :], preferred_element_type=F32)
        on = ot * lax.rsqrt(jnp.mean(ot * ot, axis=0, keepdims=True) + EPS)
        on = on * gain
        z = zt_ref[:, r0:r0 + Q_TILE]
        yt_ref[:, r0:r0 + Q_TILE] = (on * (z * _sigmoid(z))).astype(BF16)

    tiles = list(range(0, seq, Q_TILE))
    ahead = scores(tiles[0])
    for i, r0 in enumerate(tiles):
        current = ahead
        if i + 1 < len(tiles):
            ahead = scores(tiles[i + 1])
        if i:
            output(tiles[i - 1])
        weights(r0, current)
    output(tiles[-1])


def _attn(q, k, vt, zt, lq1, lk1, lq2, lk2, subln_g, lam_init, batch, seq):
    t, w = q.shape
    hw = 2 * DIFF_HEAD_DIM
    n_heads = w // hw
    assert seq % Q_TILE == 0 and Q_TILE % KEY_CHUNK == 0 and t == batch * seq
    rows = pl.BlockSpec((seq, hw), lambda b, h: (b, h))
    cols = pl.BlockSpec((hw, seq), lambda b, h: (h, b))
    vec = _resident((1, DIFF_HEAD_DIM))
    return pl.pallas_call(
        functools.partial(_attn_kernel, lam_init=lam_init),
        grid=(batch, n_heads),
        in_specs=[vec, vec, vec, vec, _resident((hw, 1)), rows, rows, cols, cols],
        out_specs=cols,
        out_shape=jax.ShapeDtypeStruct((w, t), BF16),
        scratch_shapes=[
            pltpu.VMEM((TILE_SLOTS, 2, seq, Q_TILE), F32),
            pltpu.VMEM((TILE_SLOTS, seq, Q_TILE), BF16),
        ],
        compiler_params=pltpu.CompilerParams(
            dimension_semantics=("arbitrary", "arbitrary"), vmem_limit_bytes=VMEM_LIMIT_BYTES),
        name="attn",
    )(lq1.reshape(1, -1), lk1.reshape(1, -1), lq2.reshape(1, -1), lk2.reshape(1, -1),
      subln_g.reshape(hw, 1), q, k, vt, zt)


def _out_proj_kernel(h_ref, yt_ref, wot_ref, fg_ref, o_ref):
    ot = jnp.dot(wot_ref[...], yt_ref[...], preferred_element_type=F32)
    h = h_ref[...] + ot.T
    hn = h * lax.rsqrt(jnp.mean(h * h, axis=-1, keepdims=True) + EPS)
    o_ref[...] = hn * fg_ref[...]


def _out_proj(h, yt, w_o, final_g):
    t, d = h.shape
    w = yt.shape[0]
    return pl.pallas_call(
        _out_proj_kernel,
        grid=(t // TOKEN_TILE,),
        in_specs=[
            pl.BlockSpec((TOKEN_TILE, d), lambda i: (i, 0)),
            pl.BlockSpec((w, TOKEN_TILE), lambda i: (0, i)),
            _resident((d, w)),
            _resident((1, d)),
        ],
        out_specs=pl.BlockSpec((TOKEN_TILE, d), lambda i: (i, 0)),
        out_shape=jax.ShapeDtypeStruct((t, d), F32),
        compiler_params=pltpu.CompilerParams(
            dimension_semantics=("arbitrary",), vmem_limit_bytes=VMEM_LIMIT_BYTES),
        name="out_proj",
    )(h, yt, w_o.T.astype(BF16), final_g.reshape(1, d))


def _lambda_init(layer_idx):
    return 0.8 - 0.6 * math.exp(-0.3 * layer_idx)


def kernel(x, a_norm_g, a_w_in, a_ln_g, a_ln_b, a_w_s, a_b_s, a_w_out,
           b_norm_g, b_w_qz, b_lam_q1, b_lam_k1, b_lam_q2, b_lam_k2, b_subln_g, b_w_o,
           kv_norm_g, w_kv, final_g):
    batch, seq, d = x.shape
    n_a, n_b = a_norm_g.shape[0], b_norm_g.shape[0]
    assert n_a == 1 and n_b == 1, "one gMLP layer followed by one attention layer"
    h = x.reshape(batch * seq, d)
    h = _layer_a(h, a_norm_g[0], a_w_in[0], a_ln_g[0], a_ln_b[0], a_w_s[0], a_b_s[0], a_w_out[0])
    k, q, vt, zt = _proj(h, kv_norm_g, b_norm_g[0], w_kv, b_w_qz[0])
    yt = _attn(q, k, vt, zt, b_lam_q1[0], b_lam_k1[0], b_lam_q2[0], b_lam_k2[0], b_subln_g[0],
               _lambda_init(n_a), batch, seq)
    out = _out_proj(h, yt, b_w_o[0], final_g)
    return out.reshape(batch, seq, d)
```

```python
import functools
import math

import jax
import jax.numpy as jnp
from jax import lax
from jax.experimental import pallas as pl
from jax.experimental.pallas import tpu as pltpu

F32 = jnp.float32
BF16 = jnp.bfloat16

CHUNK = 128
GROUP_W = 128
DIFF_HEAD_DIM = 128
EPS = 1e-6
SUBLANES = 8
LANES = 128
QK_EXP2_SCALE = DIFF_HEAD_DIM ** -0.5 * math.log2(math.e)

LAYER_A_TILE = 512
PROJ_TILE = 512
OUT_PROJ_TILE = 1024
GATE_COLS = 1024
Q_TILE = 256
KEY_CHUNK = 256
TILE_SLOTS = 2
VMEM_LIMIT_BYTES = 56 * 1024 * 1024

_NT = (((1,), (1,)), ((), ()))


def _sigmoid(x):
    return 1.0 / (1.0 + jnp.exp(-x))


def _resident(shape):
    return pl.BlockSpec(shape, lambda *_: (0,) * len(shape), pipeline_mode=pl.Buffered(1))


def _layer_a_kernel(x_ref, ng_ref, win_ref, lng_ref, lnb_ref, ws_ref, bs_ref, wout_ref,
                    o_ref, wsm_s, bsb_s, vn_s, y_s):
    tm, _ = x_ref.shape
    e = vn_s.shape[1]
    n_groups = ws_ref.shape[0]

    @pl.when(pl.program_id(0) == 0)
    def _():
        row = lax.broadcasted_iota(jnp.int32, (CHUNK, CHUNK), 0)
        col = lax.broadcasted_iota(jnp.int32, (CHUNK, CHUNK), 1)
        causal = col <= row
        for g in range(n_groups):
            wsm_s[g] = jnp.where(causal, ws_ref[g], 0.0).astype(BF16)
            bsb_s[g] = jnp.broadcast_to(bs_ref[g], (CHUNK, GROUP_W))

    x = x_ref[...]
    hn = x * lax.rsqrt(jnp.mean(x * x, axis=-1, keepdims=True) + EPS)
    hn = (hn * ng_ref[...]).astype(BF16)

    def project_uz(c0):
        return (jnp.dot(hn, win_ref[:, c0:c0 + GATE_COLS], preferred_element_type=F32),
                jnp.dot(hn, win_ref[:, 2 * e + c0:2 * e + c0 + GATE_COLS], preferred_element_type=F32))

    v = jnp.dot(hn, win_ref[:, e:2 * e], preferred_element_type=F32)
    blocks = list(range(0, e, GATE_COLS))
    ahead = project_uz(blocks[0])

    v = jax.nn.gelu(v)
    mu = jnp.mean(v, axis=-1, keepdims=True)
    vc = v - mu
    vn = vc * lax.rsqrt(jnp.mean(vc * vc, axis=-1, keepdims=True) + EPS)
    vn_s[...] = (vn * lng_ref[...] + lnb_ref[...]).astype(BF16)

    out = x
    for j, c0 in enumerate(blocks):
        u, z = ahead
        if j + 1 < len(blocks):
            ahead = project_uz(blocks[j + 1])
        u = jax.nn.gelu(u)
        gate = z * _sigmoid(z)
        for r0 in range(0, tm, CHUNK):
            for gc in range(0, GATE_COLS, GROUP_W):
                g = (c0 + gc) // GROUP_W
                cols = slice(c0 + gc, c0 + gc + GROUP_W)
                sv = jnp.dot(wsm_s[g], vn_s[r0:r0 + CHUNK, cols], preferred_element_type=F32)
                sv = sv + bsb_s[g]
                y = u[r0:r0 + CHUNK, gc:gc + GROUP_W] * sv * gate[r0:r0 + CHUNK, gc:gc + GROUP_W]
                y_s[r0:r0 + CHUNK, cols] = y.astype(BF16)
        out = out + jnp.dot(y_s[:, c0:c0 + GATE_COLS], wout_ref[c0:c0 + GATE_COLS, :],
                            preferred_element_type=F32)
    o_ref[...] = out


def _layer_a(x, norm_g, w_in, ln_g, ln_b, w_s, b_s, w_out):
    t, d = x.shape
    e = w_out.shape[0]
    n_groups = w_s.shape[0]
    tile = LAYER_A_TILE
    assert t % tile == 0 and tile % CHUNK == 0 and e % GATE_COLS == 0
    return pl.pallas_call(
        _layer_a_kernel,
        grid=(t // tile,),
        in_specs=[
            pl.BlockSpec((tile, d), lambda i: (i, 0)),
            _resident((1, d)),
            _resident((d, 3 * e)),
            _resident((1, e)),
            _resident((1, e)),
            _resident((n_groups, CHUNK, CHUNK)),
            _resident((n_groups, CHUNK, 1)),
            _resident((e, d)),
        ],
        out_specs=pl.BlockSpec((tile, d), lambda i: (i, 0)),
        out_shape=jax.ShapeDtypeStruct((t, d), F32),
        scratch_shapes=[
            pltpu.VMEM((n_groups, CHUNK, CHUNK), BF16),
            pltpu.VMEM((n_groups, CHUNK, GROUP_W), F32),
            pltpu.VMEM((tile, e), BF16),
            pltpu.VMEM((tile, e), BF16),
        ],
        compiler_params=pltpu.CompilerParams(
            dimension_semantics=("arbitrary",), vmem_limit_bytes=VMEM_LIMIT_BYTES),
        name="layer_a",
    )(x, norm_g.reshape(1, d), w_in.astype(BF16), ln_g.reshape(1, e), ln_b.reshape(1, e),
      w_s, b_s.reshape(n_groups, CHUNK, 1), w_out.astype(BF16))


def _proj_kernel(h_ref, kvg_ref, qg_ref, wk_ref, wq_ref, wvt_ref, wzt_ref, k_ref, q_ref, vt_ref, zt_ref):
    h = h_ref[...]
    hr = h * lax.rsqrt(jnp.mean(h * h, axis=-1, keepdims=True) + EPS)
    hk = (hr * kvg_ref[...]).astype(BF16)
    hq = (hr * qg_ref[...]).astype(BF16)
    k = jnp.dot(hk, wk_ref[...], preferred_element_type=F32).astype(BF16)
    q = jnp.dot(hq, wq_ref[...], preferred_element_type=F32)
    q = (q * QK_EXP2_SCALE).astype(BF16)
    hw = k_ref.shape[2]
    for head in range(k_ref.shape[0]):
        k_ref[head] = k[:, head * hw:(head + 1) * hw]
        q_ref[head] = q[:, head * hw:(head + 1) * hw]
    vt_ref[...] = lax.dot_general(wvt_ref[...], hk, _NT, preferred_element_type=F32).astype(BF16)
    zt_ref[...] = lax.dot_general(wzt_ref[...], hq, _NT, preferred_element_type=F32)


def _proj(h, kv_norm_g, q_norm_g, w_kv, w_qz):
    t, d = h.shape
    w = w_kv.shape[1] // 2
    hw = 2 * DIFF_HEAD_DIM
    n_heads = w // hw
    tile = PROJ_TILE
    rows = pl.BlockSpec((n_heads, tile, hw), lambda i: (0, i, 0))
    cols = pl.BlockSpec((w, tile), lambda i: (0, i))
    return pl.pallas_call(
        _proj_kernel,
        grid=(t // tile,),
        in_specs=[
            pl.BlockSpec((tile, d), lambda i: (i, 0)),
            _resident((1, d)),
            _resident((1, d)),
            _resident((d, w)),
            _resident((d, w)),
            _resident((w, d)),
            _resident((w, d)),
        ],
        out_specs=[rows, rows, cols, cols],
        out_shape=[jax.ShapeDtypeStruct((n_heads, t, hw), BF16), jax.ShapeDtypeStruct((n_heads, t, hw), BF16),
                   jax.ShapeDtypeStruct((w, t), BF16), jax.ShapeDtypeStruct((w, t), F32)],
        compiler_params=pltpu.CompilerParams(
            dimension_semantics=("arbitrary",), vmem_limit_bytes=VMEM_LIMIT_BYTES),
        name="proj",
    )(h, kv_norm_g.reshape(1, d), q_norm_g.reshape(1, d),
      w_kv[:, :w].astype(BF16), w_qz[:, :w].astype(BF16),
      w_kv[:, w:].T.astype(BF16), w_qz[:, w:].T.astype(BF16))


def _attn_kernel(lq1_ref, lk1_ref, lq2_ref, lk2_ref, sg_ref, q_ref, k_ref, vt_ref, zt_ref, yt_ref,
                 ex_s, wt_s, *, lam_init):
    seq = q_ref.shape[0]
    d = DIFF_HEAD_DIM
    lam =(jnp.exp(jnp.sum(lq1_ref[...] * lk1_ref[...], keepdims=True))
           - jnp.exp(jnp.sum(lq2_ref[...] * lk2_ref[...], keepdims=True)) + lam_init)
    gain =sg_ref[...] * (1.0 - lam_init)

    groups = KEY_CHUNK // SUBLANES

    def first_visible_lane(key_offset):
        return max(key_offset, 0) // LANES * LANES

    def scores(r0):
        return [lax.dot_general(k_ref[0:r0 + Q_TILE, n * d:(n + 1) * d],
                                q_ref[r0:r0 + Q_TILE, n * d:(n + 1) * d],
                                _NT, preferred_element_type=F32) for n in range(2)]

    def weights(r0, st_pair, slot):
        p_len = r0 + Q_TILE
        chunks = range(0, p_len, KEY_CHUNK)
        coef = []
        for n, st in enumerate(st_pair):
            m_c, l_c = [], []
            for c0 in chunks:
                lane0 = first_visible_lane(c0 - r0)
                s = st[c0:c0 + KEY_CHUNK, lane0:]
                if c0 >= r0:
                    key = lax.broadcasted_iota(jnp.int32, s.shape, 0) + (c0 - r0)
                    query = lax.broadcasted_iota(jnp.int32, s.shape, 1) + lane0
                    s = jnp.where(key <= query, s, -jnp.inf)
                s = s.reshape(groups, SUBLANES, Q_TILE - lane0)
                m = jnp.max(s, axis=0)
                shift = m
                if c0 >= r0:
                    shift = jnp.where(m == -jnp.inf, 0.0, m)
                ex = jnp.exp2(s - shift)
                ex_s[n, c0:c0 + KEY_CHUNK, lane0:] = ex.reshape(KEY_CHUNK, Q_TILE - lane0)
                l = jnp.sum(ex, axis=0)
                if lane0:
                    m = jnp.concatenate([jnp.full((SUBLANES, lane0), -jnp.inf, F32), m], axis=1)
                    l = jnp.concatenate([jnp.zeros((SUBLANES, lane0), F32), l], axis=1)
                m_c.append(m)
                l_c.append(l)
            m_row = jnp.max(functools.reduce(jnp.maximum, m_c), axis=0, keepdims=True)
            m_row = jnp.broadcast_to(m_row, (SUBLANES, Q_TILE))
            alpha = [jnp.exp2(m - m_row) for m in m_c]
            l_row = jnp.sum(sum(a * l for a, l in zip(alpha, l_c)), axis=0, keepdims=True)
            inv = 1.0 / l_row
            if n == 1:
                inv = lam * inv
            inv = jnp.broadcast_to(inv, (SUBLANES, Q_TILE))
            coef.append([a * inv for a in alpha])
        for j, c0 in enumerate(chunks):
            lane0 = first_visible_lane(c0 - r0)
            part = (groups, SUBLANES, Q_TILE - lane0)
            w = (ex_s[0, c0:c0 + KEY_CHUNK, lane0:].reshape(part) * coef[0][j][:, lane0:]
                 - ex_s[1, c0:c0 + KEY_CHUNK, lane0:].reshape(part) * coef[1][j][:, lane0:])
            wt_s[slot, c0:c0 + KEY_CHUNK, lane0:] = w.reshape(KEY_CHUNK, Q_TILE - lane0).astype(BF16)
            if lane0:
                wt_s[slot, c0:c0 + KEY_CHUNK, :lane0] = jnp.zeros((KEY_CHUNK, lane0), BF16)

    def output(r0, slot):
        p_len = r0 + Q_TILE
        ot = jnp.dot(vt_ref[:, 0:p_len], wt_s[slot, 0:p_len, :], preferred_element_type=F32)
        on = ot * lax.rsqrt(jnp.mean(ot * ot, axis=0, keepdims=True) + EPS)
        on = on * gain
        z = zt_ref[:, r0:r0 + Q_TILE]
        yt_ref[:, r0:r0 + Q_TILE] = (on * (z * _sigmoid(z))).astype(BF16)

    tiles = list(range(0, seq, Q_TILE))
    tiles = tiles[1::2] + tiles[::2][::-1]
    ahead = scores(tiles[0])
    for i, r0 in enumerate(tiles):
        current = ahead
        if i + 1 < len(tiles):
            ahead = scores(tiles[i + 1])
        if i:
            output(tiles[i - 1], (i - 1) % TILE_SLOTS)
        weights(r0, current, i % TILE_SLOTS)
    output(tiles[-1], (len(tiles) - 1) % TILE_SLOTS)


def _attn(q, k, vt, zt, lq1, lk1, lq2, lk2, subln_g, lam_init, batch, seq):
    n_heads, t, hw = q.shape
    w = n_heads * hw
    assert seq % Q_TILE == 0 and Q_TILE % KEY_CHUNK == 0 and t == batch * seq
    rows = pl.BlockSpec((None, seq, hw), lambda b, h: (h, b, 0))
    cols = pl.BlockSpec((hw, seq), lambda b, h: (h, b))
    vec = _resident((1, DIFF_HEAD_DIM))
    return pl.pallas_call(
        functools.partial(_attn_kernel, lam_init=lam_init),
        grid=(batch, n_heads),
        in_specs=[vec, vec, vec, vec, _resident((hw, 1)), rows, rows, cols, cols],
        out_specs=cols,
        out_shape=jax.ShapeDtypeStruct((w, t), BF16),
        scratch_shapes=[
            pltpu.VMEM((2, seq, Q_TILE), F32),
            pltpu.VMEM((TILE_SLOTS, seq, Q_TILE), BF16),
        ],
        compiler_params=pltpu.CompilerParams(
            dimension_semantics=("arbitrary", "arbitrary"), vmem_limit_bytes=VMEM_LIMIT_BYTES),
        name="attn",
    )(lq1.reshape(1, -1), lk1.reshape(1, -1), lq2.reshape(1, -1), lk2.reshape(1, -1),
      subln_g.reshape(hw, 1), q, k, vt, zt)


def _out_proj_kernel(h_ref, yt_ref, wot_ref, fg_ref, o_ref):
    ot = jnp.dot(wot_ref[...], yt_ref[...], preferred_element_type=F32)
    h = h_ref[...] + ot.T
    hn = h * lax.rsqrt(jnp.mean(h * h, axis=-1, keepdims=True) + EPS)
    o_ref[...] = hn * fg_ref[...]


def _out_proj(h, yt, w_o, final_g):
    t, d = h.shape
    w = yt.shape[0]
    tile = OUT_PROJ_TILE
    return pl.pallas_call(
        _out_proj_kernel,
        grid=(t // tile,),
        in_specs=[
            pl.BlockSpec((tile, d), lambda i: (i, 0)),
            pl.BlockSpec((w, tile), lambda i: (0, i)),
            _resident((d, w)),
            _resident((1, d)),
        ],
        out_specs=pl.BlockSpec((tile, d), lambda i: (i, 0)),
        out_shape=jax.ShapeDtypeStruct((t, d), F32),
        compiler_params=pltpu.CompilerParams(
            dimension_semantics=("arbitrary",), vmem_limit_bytes=VMEM_LIMIT_BYTES),
        name="out_proj",
    )(h, yt, w_o.T.astype(BF16), final_g.reshape(1, d))


def _lambda_init(layer_idx):
    return 0.8 - 0.6 * math.exp(-0.3 * layer_idx)


def kernel(x, a_norm_g, a_w_in, a_ln_g, a_ln_b, a_w_s, a_b_s, a_w_out,
           b_norm_g, b_w_qz, b_lam_q1, b_lam_k1, b_lam_q2, b_lam_k2, b_subln_g, b_w_o,
           kv_norm_g, w_kv, final_g):
    batch, seq, d = x.shape
    n_a, n_b = a_norm_g.shape[0], b_norm_g.shape[0]
    assert n_a == 1 and n_b == 1, "one gMLP layer followed by one attention layer"
    h = x.reshape(batch * seq, d)
    h = _layer_a(h, a_norm_g[0], a_w_in[0], a_ln_g[0], a_ln_b[0], a_w_s[0], a_b_s[0], a_w_out[0])
    k, q, vt, zt = _proj(h, kv_norm_g, b_norm_g[0], w_kv, b_w_qz[0])
    yt = _attn(q, k, vt, zt, b_lam_q1[0], b_lam_k1[0], b_lam_q2[0], b_lam_k2[0], b_subln_g[0],
               _lambda_init(n_a), batch, seq)
    out = _out_proj(h, yt, b_w_o[0], final_g)
    return out.reshape(batch, seq, d)
```

```python
import functools
import math

import jax
import jax.numpy as jnp
from jax import lax
from jax.experimental import pallas as pl
from jax.experimental.pallas import tpu as pltpu

F32 = jnp.float32
BF16 = jnp.bfloat16

CHUNK = 128
GROUP_W = 128
DIFF_HEAD_DIM = 128
EPS = 1e-6
SUBLANES = 8
LANES = 128
QK_EXP2_SCALE = DIFF_HEAD_DIM ** -0.5 * math.log2(math.e)

LAYER_A_TILE = 512
PROJ_TILE = 512
OUT_PROJ_TILE = 1024
GATE_COLS = 1024
Q_TILE = 256
KEY_CHUNK = 256
TILE_SLOTS = 2
VMEM_LIMIT_BYTES = 56 * 1024 * 1024

_NT = (((1,), (1,)), ((), ()))


def _silu(x):
    h = 0.5 * x
    return h + h * jnp.tanh(h)


def _gelu_tanh(x):
    c = math.sqrt(2.0 / math.pi)
    h = 0.5 * x
    return h + h * jnp.tanh(x * (c + (c * 0.044715) * (x * x)))


def _resident(shape):
    return pl.BlockSpec(shape, lambda *_: (0,) * len(shape), pipeline_mode=pl.Buffered(1))


def _layer_a_kernel(x_ref, ng_ref, win_ref, lng_ref, lnb_ref, ws_ref, bs_ref, wout_ref,
                    o_ref, wsm_s, bsb_s, vn_s, y_s):
    tm, _ = x_ref.shape
    e = vn_s.shape[1]
    n_groups = ws_ref.shape[0]

    @pl.when(pl.program_id(0) == 0)
    def _():
        row = lax.broadcasted_iota(jnp.int32, (CHUNK, CHUNK), 0)
        col = lax.broadcasted_iota(jnp.int32, (CHUNK, CHUNK), 1)
        causal = col <= row
        for g in range(n_groups):
            wsm_s[g] = jnp.where(causal, ws_ref[g], 0.0).astype(BF16)
            bsb_s[g] = jnp.broadcast_to(bs_ref[g], (CHUNK, GROUP_W))

    x = x_ref[...]
    hn = x * lax.rsqrt(jnp.mean(x * x, axis=-1, keepdims=True) + EPS)
    hn = (hn * ng_ref[...]).astype(BF16)

    def project_uz(c0):
        return (jnp.dot(hn, win_ref[:, c0:c0 + GATE_COLS], preferred_element_type=F32),
                jnp.dot(hn, win_ref[:, 2 * e + c0:2 * e + c0 + GATE_COLS], preferred_element_type=F32))

    v = jnp.dot(hn, win_ref[:, e:2 * e], preferred_element_type=F32)
    blocks = list(range(0, e, GATE_COLS))
    ahead = project_uz(blocks[0])

    v = _gelu_tanh(v)
    mu = jnp.mean(v, axis=-1, keepdims=True)
    vc = v - mu
    vn = vc * lax.rsqrt(jnp.mean(vc * vc, axis=-1, keepdims=True) + EPS)
    vn_s[...] = (vn * lng_ref[...] + lnb_ref[...]).astype(BF16)

    out = x
    for j, c0 in enumerate(blocks):
        u, z = ahead
        if j + 1 < len(blocks):
            ahead = project_uz(blocks[j + 1])
        u = _gelu_tanh(u)
        gate = _silu(z)
        for r0 in range(0, tm, CHUNK):
            for gc in range(0, GATE_COLS, GROUP_W):
                g = (c0 + gc) // GROUP_W
                cols = slice(c0 + gc, c0 + gc + GROUP_W)
                sv = jnp.dot(wsm_s[g], vn_s[r0:r0 + CHUNK, cols], preferred_element_type=F32)
                sv = sv + bsb_s[g]
                y = u[r0:r0 + CHUNK, gc:gc + GROUP_W] * sv * gate[r0:r0 + CHUNK, gc:gc + GROUP_W]
                y_s[r0:r0 + CHUNK, cols] = y.astype(BF16)
        out = out + jnp.dot(y_s[:, c0:c0 + GATE_COLS], wout_ref[c0:c0 + GATE_COLS, :],
                            preferred_element_type=F32)
    o_ref[...] = out


def _layer_a(x, norm_g, w_in, ln_g, ln_b, w_s, b_s, w_out):
    t, d = x.shape
    e = w_out.shape[0]
    n_groups = w_s.shape[0]
    tile = LAYER_A_TILE
    assert t % tile == 0 and tile % CHUNK == 0 and e % GATE_COLS == 0
    return pl.pallas_call(
        _layer_a_kernel,
        grid=(t // tile,),
        in_specs=[
            pl.BlockSpec((tile, d), lambda i: (i, 0)),
            _resident((1, d)),
            _resident((d, 3 * e)),
            _resident((1, e)),
            _resident((1, e)),
            _resident((n_groups, CHUNK, CHUNK)),
            _resident((n_groups, CHUNK, 1)),
            _resident((e, d)),
        ],
        out_specs=pl.BlockSpec((tile, d), lambda i: (i, 0)),
        out_shape=jax.ShapeDtypeStruct((t, d), F32),
        scratch_shapes=[
            pltpu.VMEM((n_groups, CHUNK, CHUNK), BF16),
            pltpu.VMEM((n_groups, CHUNK, GROUP_W), F32),
            pltpu.VMEM((tile, e), BF16),
            pltpu.VMEM((tile, e), BF16),
        ],
        compiler_params=pltpu.CompilerParams(
            dimension_semantics=("arbitrary",), vmem_limit_bytes=VMEM_LIMIT_BYTES),
        name="layer_a",
    )(x, norm_g.reshape(1, d), w_in.astype(BF16), ln_g.reshape(1, e), ln_b.reshape(1, e),
      w_s, b_s.reshape(n_groups, CHUNK, 1), w_out.astype(BF16))


def _proj_kernel(h_ref, kvg_ref, qg_ref, wk_ref, wq_ref, wvt_ref, wzt_ref, k_ref, q_ref, vt_ref, zt_ref):
    h = h_ref[...]
    hr = h * lax.rsqrt(jnp.mean(h * h, axis=-1, keepdims=True) + EPS)
    hk = (hr * kvg_ref[...]).astype(BF16)
    hq = (hr * qg_ref[...]).astype(BF16)
    k = jnp.dot(hk, wk_ref[...], preferred_element_type=F32).astype(BF16)
    q = jnp.dot(hq, wq_ref[...], preferred_element_type=F32)
    q = (q * QK_EXP2_SCALE).astype(BF16)
    hw = k_ref.shape[2]
    for head in range(k_ref.shape[0]):
        k_ref[head] = k[:, head * hw:(head + 1) * hw]
        q_ref[head] = q[:, head * hw:(head + 1) * hw]
    vt_ref[...] = lax.dot_general(wvt_ref[...], hk, _NT, preferred_element_type=F32).astype(BF16)
    zt_ref[...] = lax.dot_general(wzt_ref[...], hq, _NT, preferred_element_type=F32)


def _proj(h, kv_norm_g, q_norm_g, w_kv, w_qz):
    t, d = h.shape
    w = w_kv.shape[1] // 2
    hw = 2 * DIFF_HEAD_DIM
    n_heads = w // hw
    tile = PROJ_TILE
    rows = pl.BlockSpec((n_heads, tile, hw), lambda i: (0, i, 0))
    cols = pl.BlockSpec((w, tile), lambda i: (0, i))
    return pl.pallas_call(
        _proj_kernel,
        grid=(t // tile,),
        in_specs=[
            pl.BlockSpec((tile, d), lambda i: (i, 0)),
            _resident((1, d)),
            _resident((1, d)),
            _resident((d, w)),
            _resident((d, w)),
            _resident((w, d)),
            _resident((w, d)),
        ],
        out_specs=[rows, rows, cols, cols],
        out_shape=[jax.ShapeDtypeStruct((n_heads, t, hw), BF16), jax.ShapeDtypeStruct((n_heads, t, hw), BF16),
                   jax.ShapeDtypeStruct((w, t), BF16), jax.ShapeDtypeStruct((w, t), F32)],
        compiler_params=pltpu.CompilerParams(
            dimension_semantics=("arbitrary",), vmem_limit_bytes=VMEM_LIMIT_BYTES),
        name="proj",
    )(h, kv_norm_g.reshape(1, d), q_norm_g.reshape(1, d),
      w_kv[:, :w].astype(BF16), w_qz[:, :w].astype(BF16),
      w_kv[:, w:].astype(BF16).T, w_qz[:, w:].astype(BF16).T)


def _attn_kernel(lq1_ref, lk1_ref, lq2_ref, lk2_ref, sg_ref, q_ref, k_ref, vt_ref, zt_ref, yt_ref,
                 ex_s, wt_s, *, lam_init):
    seq = q_ref.shape[0]
    d = DIFF_HEAD_DIM
    lam = (jnp.exp(jnp.sum(lq1_ref[...] * lk1_ref[...], keepdims=True))
           - jnp.exp(jnp.sum(lq2_ref[...] * lk2_ref[...], keepdims=True)) + lam_init)
    gain = sg_ref[...] * (1.0 - lam_init)

    groups = KEY_CHUNK // SUBLANES

    def first_visible_lane(key_offset):
        return max(key_offset, 0) // LANES * LANES

    def scores(r0):
        return [lax.dot_general(k_ref[0:r0 + Q_TILE, n * d:(n + 1) * d],
                                q_ref[r0:r0 + Q_TILE, n * d:(n + 1) * d],
                                _NT, preferred_element_type=F32) for n in range(2)]

    def weights(r0, st_pair, slot):
        p_len = r0 + Q_TILE
        chunks = range(0, p_len, KEY_CHUNK)
        coef = []
        for n, st in enumerate(st_pair):
            m_c, l_c = [], []
            for c0 in chunks:
                lane0 = first_visible_lane(c0 - r0)
                s = st[c0:c0 + KEY_CHUNK, lane0:]
                if c0 >= r0:
                    key = lax.broadcasted_iota(jnp.int32, s.shape, 0) + (c0 - r0)
                    query = lax.broadcasted_iota(jnp.int32, s.shape, 1) + lane0
                    s = jnp.where(key <= query, s, -jnp.inf)
                s = s.reshape(groups, SUBLANES, Q_TILE - lane0)
                m = jnp.max(s, axis=0)
                shift = m
                if c0 >= r0:
                    shift = jnp.where(m == -jnp.inf, 0.0, m)
                ex = jnp.exp2(s - shift)
                ex_s[n, c0:c0 + KEY_CHUNK, lane0:] = ex.reshape(KEY_CHUNK, Q_TILE - lane0)
                l = jnp.sum(ex, axis=0)
                if lane0:
                    m = jnp.concatenate([jnp.full((SUBLANES, lane0), -jnp.inf, F32), m], axis=1)
                    l = jnp.concatenate([jnp.zeros((SUBLANES, lane0), F32), l], axis=1)
                m_c.append(m)
                l_c.append(l)
            m_row = jnp.max(functools.reduce(jnp.maximum, m_c), axis=0, keepdims=True)
            m_row = jnp.broadcast_to(m_row, (SUBLANES, Q_TILE))
            alpha = [jnp.exp2(m - m_row) for m in m_c]
            l_row = jnp.sum(sum(a * l for a, l in zip(alpha, l_c)), axis=0, keepdims=True)
            inv = 1.0 / l_row
            if n == 1:
                inv = lam * inv
            inv = jnp.broadcast_to(inv, (SUBLANES, Q_TILE))
            coef.append([a * inv for a in alpha])
        for j, c0 in enumerate(chunks):
            lane0 = first_visible_lane(c0 - r0)
            part = (groups, SUBLANES, Q_TILE - lane0)
            w = (ex_s[0, c0:c0 + KEY_CHUNK, lane0:].reshape(part) * coef[0][j][:, lane0:]
                 - ex_s[1, c0:c0 + KEY_CHUNK, lane0:].reshape(part) * coef[1][j][:, lane0:])
            wt_s[slot, c0:c0 + KEY_CHUNK, lane0:] = w.reshape(KEY_CHUNK, Q_TILE - lane0).astype(BF16)
            if lane0:
                wt_s[slot, c0:c0 + KEY_CHUNK, :lane0] = jnp.zeros((KEY_CHUNK, lane0), BF16)

    def output(r0, slot):
        p_len = r0 + Q_TILE
        ot = jnp.dot(vt_ref[:, 0:p_len], wt_s[slot, 0:p_len, :], preferred_element_type=F32)
        on = ot * lax.rsqrt(jnp.mean(ot * ot, axis=0, keepdims=True) + EPS)
        on = on * gain
        z = zt_ref[:, r0:r0 + Q_TILE]
        yt_ref[:, r0:r0 + Q_TILE] = (on * _silu(z)).astype(BF16)

    tiles = list(range(0, seq, Q_TILE))
    tiles = tiles[::2] + tiles[1::2][::-1]
    ahead = scores(tiles[0])
    for i, r0 in enumerate(tiles):
        current = ahead
        if i + 1 < len(tiles):
            ahead = scores(tiles[i + 1])
        if i:
            output(tiles[i - 1], (i - 1) % TILE_SLOTS)
        weights(r0, current, i % TILE_SLOTS)
    output(tiles[-1], (len(tiles) - 1) % TILE_SLOTS)


def _attn(q, k, vt, zt, lq1, lk1, lq2, lk2, subln_g, lam_init, batch, seq):
    n_heads, t, hw = q.shape
    w = n_heads * hw
    assert seq % Q_TILE == 0 and Q_TILE % KEY_CHUNK == 0 and t == batch * seq
    rows = pl.BlockSpec((None, seq, hw), lambda b, h: (h, b, 0))
    cols = pl.BlockSpec((hw, seq), lambda b, h: (h, b))
    vec = _resident((1, DIFF_HEAD_DIM))
    return pl.pallas_call(
        functools.partial(_attn_kernel, lam_init=lam_init),
        grid=(batch, n_heads),
        in_specs=[vec, vec, vec, vec, _resident((hw, 1)), rows, rows, cols, cols],
        out_specs=cols,
        out_shape=jax.ShapeDtypeStruct((w, t), BF16),
        scratch_shapes=[
            pltpu.VMEM((2, seq, Q_TILE), F32),
            pltpu.VMEM((TILE_SLOTS, seq, Q_TILE), BF16),
        ],
        compiler_params=pltpu.CompilerParams(
            dimension_semantics=("arbitrary", "arbitrary"), vmem_limit_bytes=VMEM_LIMIT_BYTES),
        name="attn",
    )(lq1.reshape(1, -1), lk1.reshape(1, -1), lq2.reshape(1, -1), lk2.reshape(1, -1),
      subln_g.reshape(hw, 1), q, k, vt, zt)


def _out_proj_kernel(h_ref, yt_ref, wot_ref, fg_ref, o_ref):
    ot = jnp.dot(wot_ref[...], yt_ref[...], preferred_element_type=F32)
    h = h_ref[...] + ot.T
    hn = h * lax.rsqrt(jnp.mean(h * h, axis=-1, keepdims=True) + EPS)
    o_ref[...] = hn * fg_ref[...]


def _out_proj(h, yt, w_o, final_g):
    t, d = h.shape
    w = yt.shape[0]
    tile = OUT_PROJ_TILE
    return pl.pallas_call(
        _out_proj_kernel,
        grid=(t // tile,),
        in_specs=[
            pl.BlockSpec((tile, d), lambda i: (i, 0)),
            pl.BlockSpec((w, tile), lambda i: (0, i)),
            _resident((d, w)),
            _resident((1, d)),
        ],
        out_specs=pl.BlockSpec((tile, d), lambda i: (i, 0)),
        out_shape=jax.ShapeDtypeStruct((t, d), F32),
        compiler_params=pltpu.CompilerParams(
            dimension_semantics=("arbitrary",), vmem_limit_bytes=VMEM_LIMIT_BYTES),
        name="out_proj",
    )(h, yt, w_o.astype(BF16).T, final_g.reshape(1, d))


def _lambda_init(layer_idx):
    return 0.8 - 0.6 * math.exp(-0.3 * layer_idx)


def kernel(x, a_norm_g, a_w_in, a_ln_g, a_ln_b, a_w_s, a_b_s, a_w_out,
           b_norm_g, b_w_qz, b_lam_q1, b_lam_k1, b_lam_q2, b_lam_k2, b_subln_g, b_w_o,
           kv_norm_g, w_kv, final_g):
    batch, seq, d = x.shape
    n_a, n_b = a_norm_g.shape[0], b_norm_g.shape[0]
    assert n_a == 1 and n_b == 1, "one gMLP layer followed by one attention layer"
    h = x.reshape(batch * seq, d)
    h = _layer_a(h, a_norm_g[0], a_w_in[0], a_ln_g[0], a_ln_b[0], a_w_s[0], a_b_s[0], a_w_out[0])
    k, q, vt, zt = _proj(h, kv_norm_g, b_norm_g[0], w_kv, b_w_qz[0])
    yt = _attn(q, k, vt, zt, b_lam_q1[0], b_lam_k1[0], b_lam_q2[0], b_lam_k2[0], b_subln_g[0],
               _lambda_init(n_a), batch, seq)
    out = _out_proj(h, yt, b_w_o[0], final_g)
    return out.reshape(batch, seq, d)
```

```python
import functools
import math

import jax
import jax.numpy as jnp
from jax import lax
from jax.experimental import pallas as pl
from jax.experimental.pallas import tpu as pltpu

F32 = jnp.float32
BF16 = jnp.bfloat16

CHUNK = 128
GROUP_W = 128
DIFF_HEAD_DIM = 128
EPS = 1e-6
SUBLANES = 8
LANES = 128
QK_EXP2_SCALE = DIFF_HEAD_DIM ** -0.5 * math.log2(math.e)

LAYER_A_TILE = 512
PROJ_TILE = 512
OUT_PROJ_TILE = 1024
GATE_COLS = 1024
Q_TILE = 256
KEY_CHUNK = 256
TILE_SLOTS = 2
VMEM_LIMIT_BYTES = 56 * 1024 * 1024

_NT = (((1,), (1,)), ((), ()))


def _silu(x):
    h = 0.5 * x
    return h + h * jnp.tanh(h)


def _gelu_tanh(x):
    c = math.sqrt(2.0 / math.pi)
    h = 0.5 * x
    return h + h * jnp.tanh(x * (c + (c * 0.044715) * (x * x)))


def _resident(shape):
    return pl.BlockSpec(shape, lambda *_: (0,) * len(shape), pipeline_mode=pl.Buffered(1))


def _layer_a_kernel(x_ref, ng_ref, win_ref, lng_ref, lnb_ref, ws_ref, bs_ref, wout_ref,
                    o_ref, wsm_s, bsb_s, vn_s, y_s):
    tm, _ = x_ref.shape
    e = vn_s.shape[1]
    n_groups = ws_ref.shape[0]

    @pl.when(pl.program_id(0) == 0)
    def _():
        row = lax.broadcasted_iota(jnp.int32, (CHUNK, CHUNK), 0)
        col = lax.broadcasted_iota(jnp.int32, (CHUNK, CHUNK), 1)
        causal = col <= row
        for g in range(n_groups):
            wsm_s[g] = jnp.where(causal, ws_ref[g], 0.0).astype(BF16)
            bsb_s[g] = jnp.broadcast_to(bs_ref[g], (CHUNK, GROUP_W))

    x = x_ref[...]
    hn = x * lax.rsqrt(jnp.mean(x * x, axis=-1, keepdims=True) + EPS)
    hn = (hn * ng_ref[...]).astype(BF16)

    def project_uz(c0):
        return (jnp.dot(hn, win_ref[:, c0:c0 + GATE_COLS], preferred_element_type=F32),
                jnp.dot(hn, win_ref[:, 2 * e + c0:2 * e + c0 + GATE_COLS], preferred_element_type=F32))

    v = jnp.dot(hn, win_ref[:, e:2 * e], preferred_element_type=F32)
    blocks = list(range(0, e, GATE_COLS))
    ahead = project_uz(blocks[0])

    v = _gelu_tanh(v)
    mu = jnp.mean(v, axis=-1, keepdims=True)
    vc = v - mu
    vn = vc * lax.rsqrt(jnp.mean(vc * vc, axis=-1, keepdims=True) + EPS)
    vn_s[...] = (vn * lng_ref[...] + lnb_ref[...]).astype(BF16)

    out = x
    for j, c0 in enumerate(blocks):
        u, z = ahead
        if j + 1 < len(blocks):
            ahead = project_uz(blocks[j + 1])
        u = _gelu_tanh(u)
        gate = _silu(z)
        for r0 in range(0, tm, CHUNK):
            for gc in range(0, GATE_COLS, GROUP_W):
                g = (c0 + gc) // GROUP_W
                cols = slice(c0 + gc, c0 + gc + GROUP_W)
                sv = jnp.dot(wsm_s[g], vn_s[r0:r0 + CHUNK, cols], preferred_element_type=F32)
                sv = sv + bsb_s[g]
                y = u[r0:r0 + CHUNK, gc:gc + GROUP_W] * sv * gate[r0:r0 + CHUNK, gc:gc + GROUP_W]
                y_s[r0:r0 + CHUNK, cols] = y.astype(BF16)
        out = out + jnp.dot(y_s[:, c0:c0 + GATE_COLS], wout_ref[c0:c0 + GATE_COLS, :],
                            preferred_element_type=F32)
    o_ref[...] = out


def _layer_a(x, norm_g, w_in, ln_g, ln_b, w_s, b_s, w_out):
    t, d = x.shape
    e = w_out.shape[0]
    n_groups = w_s.shape[0]
    tile = LAYER_A_TILE
    assert t % tile == 0 and tile % CHUNK == 0 and e % GATE_COLS == 0
    return pl.pallas_call(
        _layer_a_kernel,
        grid=(t // tile,),
        in_specs=[
            pl.BlockSpec((tile, d), lambda i: (i, 0)),
            _resident((1, d)),
            _resident((d, 3 * e)),
            _resident((1, e)),
            _resident((1, e)),
            _resident((n_groups, CHUNK, CHUNK)),
            _resident((n_groups, CHUNK, 1)),
            _resident((e, d)),
        ],
        out_specs=pl.BlockSpec((tile, d), lambda i: (i, 0)),
        out_shape=jax.ShapeDtypeStruct((t, d), F32),
        scratch_shapes=[
            pltpu.VMEM((n_groups, CHUNK, CHUNK), BF16),
            pltpu.VMEM((n_groups, CHUNK, GROUP_W), F32),
            pltpu.VMEM((tile, e), BF16),
            pltpu.VMEM((tile, e), BF16),
        ],
        compiler_params=pltpu.CompilerParams(
            dimension_semantics=("arbitrary",), vmem_limit_bytes=VMEM_LIMIT_BYTES),
        name="layer_a",
    )(x, norm_g.reshape(1, d), w_in.astype(BF16), ln_g.reshape(1, e), ln_b.reshape(1, e),
      w_s, b_s.reshape(n_groups, CHUNK, 1), w_out.astype(BF16))


def _proj_kernel(h_ref, kvg_ref, qg_ref, wk_ref, wq_ref, wvt_ref, k_ref, q_ref, vt_ref, hq_ref):
    h = h_ref[...]
    hr = h * lax.rsqrt(jnp.mean(h * h, axis=-1, keepdims=True) + EPS)
    hk = (hr * kvg_ref[...]).astype(BF16)
    hq = (hr * qg_ref[...]).astype(BF16)
    hq_ref[...] = hq
    k = jnp.dot(hk, wk_ref[...], preferred_element_type=F32).astype(BF16)
    q = jnp.dot(hq, wq_ref[...], preferred_element_type=F32)
    q = (q * QK_EXP2_SCALE).astype(BF16)
    hw = k_ref.shape[2]
    for head in range(k_ref.shape[0]):
        k_ref[head] = k[:, head * hw:(head + 1) * hw]
        q_ref[head] = q[:, head * hw:(head + 1) * hw]
    vt_ref[...] = lax.dot_general(wvt_ref[...], hk, _NT, preferred_element_type=F32).astype(BF16)


def _proj(h, kv_norm_g, q_norm_g, w_kv, w_q):
    t, d = h.shape
    w = w_kv.shape[1] // 2
    hw = 2 * DIFF_HEAD_DIM
    n_heads = w // hw
    tile = PROJ_TILE
    rows = pl.BlockSpec((n_heads, tile, hw), lambda i: (0, i, 0))
    cols = pl.BlockSpec((w, tile), lambda i: (0, i))
    return pl.pallas_call(
        _proj_kernel,
        grid=(t // tile,),
        in_specs=[
            pl.BlockSpec((tile, d), lambda i: (i, 0)),
            _resident((1, d)),
            _resident((1, d)),
            _resident((d, w)),
            _resident((d, w)),
            _resident((w, d)),
        ],
        out_specs=[rows, rows, cols, pl.BlockSpec((tile, d), lambda i: (i, 0))],
        out_shape=[jax.ShapeDtypeStruct((n_heads, t, hw), BF16), jax.ShapeDtypeStruct((n_heads, t, hw), BF16),
                   jax.ShapeDtypeStruct((w, t), BF16), jax.ShapeDtypeStruct((t, d), BF16)],
        compiler_params=pltpu.CompilerParams(
            dimension_semantics=("arbitrary",), vmem_limit_bytes=VMEM_LIMIT_BYTES),
        name="proj",
    )(h, kv_norm_g.reshape(1, d), q_norm_g.reshape(1, d),
      w_kv[:, :w].astype(BF16), w_q.astype(BF16), w_kv[:, w:].astype(BF16).T)


def _attn_kernel(lq1_ref, lk1_ref, lq2_ref, lk2_ref, sg_ref, q_ref, k_ref, vt_ref, hq_ref, wz_ref, yt_ref,
                 ex_s, wt_s, *, lam_init):
    seq = q_ref.shape[0]
    d = DIFF_HEAD_DIM
    lam = (jnp.exp(jnp.sum(lq1_ref[...] * lk1_ref[...], keepdims=True))
           - jnp.exp(jnp.sum(lq2_ref[...] * lk2_ref[...], keepdims=True)) + lam_init)
    gain = sg_ref[...] * (1.0 - lam_init)

    groups = KEY_CHUNK // SUBLANES

    def first_visible_lane(key_offset):
        return max(key_offset, 0) // LANES * LANES

    def scores(r0):
        return [lax.dot_general(k_ref[0:r0 + Q_TILE, n * d:(n + 1) * d],
                                q_ref[r0:r0 + Q_TILE, n * d:(n + 1) * d],
                                _NT, preferred_element_type=F32) for n in range(2)]

    def weights(r0, st_pair, slot):
        p_len = r0 + Q_TILE
        chunks = range(0, p_len, KEY_CHUNK)
        coef = []
        for n, st in enumerate(st_pair):
            m_c, l_c = [], []
            for c0 in chunks:
                lane0 = first_visible_lane(c0 - r0)
                s = st[c0:c0 + KEY_CHUNK, lane0:]
                if c0 >= r0:
                    key = lax.broadcasted_iota(jnp.int32, s.shape, 0) + (c0 - r0)
                    query = lax.broadcasted_iota(jnp.int32, s.shape, 1) + lane0
                    s = jnp.where(key <= query, s, -jnp.inf)
                s = s.reshape(groups, SUBLANES, Q_TILE - lane0)
                m = jnp.max(s, axis=0)
                shift = m
                if c0 >= r0:
                    shift = jnp.where(m == -jnp.inf, 0.0, m)
                ex = jnp.exp2(s - shift)
                ex_s[n, c0:c0 + KEY_CHUNK, lane0:] = ex.reshape(KEY_CHUNK, Q_TILE - lane0)
                l = jnp.sum(ex, axis=0)
                if lane0:
                    m = jnp.concatenate([jnp.full((SUBLANES, lane0), -jnp.inf, F32), m], axis=1)
                    l = jnp.concatenate([jnp.zeros((SUBLANES, lane0), F32), l], axis=1)
                m_c.append(m)
                l_c.append(l)
            m_row = jnp.max(functools.reduce(jnp.maximum, m_c), axis=0, keepdims=True)
            m_row = jnp.broadcast_to(m_row, (SUBLANES, Q_TILE))
            alpha = [jnp.exp2(m - m_row) for m in m_c]
            l_row = jnp.sum(sum(a * l for a, l in zip(alpha, l_c)), axis=0, keepdims=True)
            inv = 1.0 / l_row
            if n == 1:
                inv = lam * inv
            inv = jnp.broadcast_to(inv, (SUBLANES, Q_TILE))
            coef.append([a * inv for a in alpha])
        for j, c0 in enumerate(chunks):
            lane0 = first_visible_lane(c0 - r0)
            part = (groups, SUBLANES, Q_TILE - lane0)
            w = (ex_s[0, c0:c0 + KEY_CHUNK, lane0:].reshape(part) * coef[0][j][:, lane0:]
                 - ex_s[1, c0:c0 + KEY_CHUNK, lane0:].reshape(part) * coef[1][j][:, lane0:])
            wt_s[slot, c0:c0 + KEY_CHUNK, lane0:] = w.reshape(KEY_CHUNK, Q_TILE - lane0).astype(BF16)
            if lane0:
                wt_s[slot, c0:c0 + KEY_CHUNK, :lane0] = jnp.zeros((KEY_CHUNK, lane0), BF16)

    def output(r0, slot):
        p_len = r0 + Q_TILE
        ot = jnp.dot(vt_ref[:, 0:p_len], wt_s[slot, 0:p_len, :], preferred_element_type=F32)
        on = ot * lax.rsqrt(jnp.mean(ot * ot, axis=0, keepdims=True) + EPS)
        on = on * gain
        z = jnp.dot(hq_ref[r0:r0 + Q_TILE, :], wz_ref[...], preferred_element_type=F32).T
        yt_ref[:, r0:r0 + Q_TILE] = (on * _silu(z)).astype(BF16)

    tiles = list(range(0, seq, Q_TILE))
    tiles = tiles[::2] + tiles[1::2][::-1]
    ahead = scores(tiles[0])
    for i, r0 in enumerate(tiles):
        current = ahead
        if i + 1 < len(tiles):
            ahead = scores(tiles[i + 1])
        if i:
            output(tiles[i - 1], (i - 1) % TILE_SLOTS)
        weights(r0, current, i % TILE_SLOTS)
    output(tiles[-1], (len(tiles) - 1) % TILE_SLOTS)


def _attn(q, k, vt, hq, w_z, lq1, lk1, lq2, lk2, subln_g, lam_init, batch, seq):
    n_heads, t, hw = q.shape
    w = n_heads * hw
    d_model = hq.shape[1]
    w_z = w_z.reshape(d_model, n_heads, hw).transpose(1, 0, 2).astype(BF16)
    assert seq % Q_TILE == 0 and Q_TILE % KEY_CHUNK == 0 and t == batch * seq
    rows = pl.BlockSpec((None, seq, hw), lambda b, h: (h, b, 0))
    cols = pl.BlockSpec((hw, seq), lambda b, h: (h, b))
    vec = _resident((1, DIFF_HEAD_DIM))
    return pl.pallas_call(
        functools.partial(_attn_kernel, lam_init=lam_init),
        grid=(batch, n_heads),
        in_specs=[vec, vec, vec, vec, _resident((hw, 1)), rows, rows, cols,
                  pl.BlockSpec((seq, d_model), lambda b, h: (b, 0)),
                  pl.BlockSpec((None, d_model, hw), lambda b, h: (h, 0, 0))],
        out_specs=cols,
        out_shape=jax.ShapeDtypeStruct((w, t), BF16),
        scratch_shapes=[
            pltpu.VMEM((2, seq, Q_TILE), F32),
            pltpu.VMEM((TILE_SLOTS, seq, Q_TILE), BF16),
        ],
        compiler_params=pltpu.CompilerParams(
            dimension_semantics=("arbitrary", "arbitrary"), vmem_limit_bytes=VMEM_LIMIT_BYTES),
        name="attn",
    )(lq1.reshape(1, -1), lk1.reshape(1, -1), lq2.reshape(1, -1), lk2.reshape(1, -1),
      subln_g.reshape(hw, 1), q, k, vt, hq, w_z)


def _out_proj_kernel(h_ref, yt_ref, wot_ref, fg_ref, o_ref):
    ot = jnp.dot(wot_ref[...], yt_ref[...], preferred_element_type=F32)
    h = h_ref[...] + ot.T
    hn = h * lax.rsqrt(jnp.mean(h * h, axis=-1, keepdims=True) + EPS)
    o_ref[...] = hn * fg_ref[...]


def _out_proj(h, yt, w_o, final_g):
    t, d = h.shape
    w = yt.shape[0]
    tile = OUT_PROJ_TILE
    return pl.pallas_call(
        _out_proj_kernel,
        grid=(t // tile,),
        in_specs=[
            pl.BlockSpec((tile, d), lambda i: (i, 0)),
            pl.BlockSpec((w, tile), lambda i: (0, i)),
            _resident((d, w)),
            _resident((1, d)),
        ],
        out_specs=pl.BlockSpec((tile, d), lambda i: (i, 0)),
        out_shape=jax.ShapeDtypeStruct((t, d), F32),
        compiler_params=pltpu.CompilerParams(
            dimension_semantics=("arbitrary",), vmem_limit_bytes=VMEM_LIMIT_BYTES),
        name="out_proj",
    )(h, yt, w_o.astype(BF16).T, final_g.reshape(1, d))


def _lambda_init(layer_idx):
    return 0.8 - 0.6 * math.exp(-0.3 * layer_idx)


def kernel(x, a_norm_g, a_w_in, a_ln_g, a_ln_b, a_w_s, a_b_s, a_w_out,
           b_norm_g, b_w_qz, b_lam_q1, b_lam_k1, b_lam_q2, b_lam_k2, b_subln_g, b_w_o,
           kv_norm_g, w_kv, final_g):
    batch, seq, d = x.shape
    n_a, n_b = a_norm_g.shape[0], b_norm_g.shape[0]
    assert n_a == 1 and n_b == 1, "one gMLP layer followed by one attention layer"
    h = x.reshape(batch * seq, d)
    h = _layer_a(h, a_norm_g[0], a_w_in[0], a_ln_g[0], a_ln_b[0], a_w_s[0], a_b_s[0], a_w_out[0])
    att_w = w_kv.shape[1] // 2
    k, q, vt, hq = _proj(h, kv_norm_g, b_norm_g[0], w_kv, b_w_qz[0][:, :att_w])
    yt = _attn(q, k, vt, hq, b_w_qz[0][:, att_w:], b_lam_q1[0], b_lam_k1[0], b_lam_q2[0], b_lam_k2[0], b_subln_g[0],
               _lambda_init(n_a), batch, seq)
    out = _out_proj(h, yt, b_w_o[0], final_g)
    return out.reshape(batch, seq, d)
```

```python
import functools
import math

import jax
import jax.numpy as jnp
from jax import lax
from jax.experimental import pallas as pl
from jax.experimental.pallas import tpu as pltpu

F32 = jnp.float32
BF16 = jnp.bfloat16

CHUNK = 128
GROUP_W = 128
DIFF_HEAD_DIM = 128
EPS = 1e-6
SUBLANES = 8
QK_EXP2_SCALE = DIFF_HEAD_DIM ** -0.5 * math.log2(math.e)

LAYER_A_TILE = 512
PROJ_TILE = 1024
OUT_PROJ_TILE = 1024
GATE_COLS = 1024
Q_TILE = 256
KEY_CHUNK = Q_TILE
VMEM_LIMIT_BYTES = 60 * 1024 * 1024

_NT = (((1,), (1,)), ((), ()))


def _silu(x):
    h = 0.5 * x
    return h + h * jnp.tanh(h)


def _gelu_tanh(x):
    c = math.sqrt(2.0 / math.pi)
    h = 0.5 * x
    return h + h * jnp.tanh(x * (c + (c * 0.044715) * (x * x)))


def _resident(shape):
    return pl.BlockSpec(shape, lambda *_: (0,) * len(shape), pipeline_mode=pl.Buffered(1))


def _layer_a_kernel(x_ref, ng_ref, win_ref, lng_ref, lnb_ref, ws_ref, bs_ref, wout_ref,
                    o_ref, wsm_s, bsb_s, vn_s, y_s):
    tm, _ = x_ref.shape
    e = vn_s.shape[1]
    n_groups = ws_ref.shape[0]

    @pl.when(pl.program_id(0) == 0)
    def _():
        row = lax.broadcasted_iota(jnp.int32, (CHUNK, CHUNK), 0)
        col = lax.broadcasted_iota(jnp.int32, (CHUNK, CHUNK), 1)
        causal = col <= row
        for g in range(n_groups):
            wsm_s[g] = jnp.where(causal, ws_ref[g], 0.0).astype(BF16)
            bsb_s[g] = jnp.broadcast_to(bs_ref[g], (CHUNK, GROUP_W))

    x = x_ref[...]
    hn = x * lax.rsqrt(jnp.mean(x * x, axis=-1, keepdims=True) + EPS)
    hn = (hn * ng_ref[...]).astype(BF16)

    def project_uz(c0):
        return (jnp.dot(hn, win_ref[:, c0:c0 + GATE_COLS], preferred_element_type=F32),
                jnp.dot(hn, win_ref[:, 2 * e + c0:2 * e + c0 + GATE_COLS], preferred_element_type=F32))

    v = jnp.dot(hn, win_ref[:, e:2 * e], preferred_element_type=F32)
    blocks = list(range(0, e, GATE_COLS))
    ahead = project_uz(blocks[0])

    v = _gelu_tanh(v)
    mu = jnp.mean(v, axis=-1, keepdims=True)
    vc = v - mu
    vn = vc * lax.rsqrt(jnp.mean(vc * vc, axis=-1, keepdims=True) + EPS)
    vn_s[...] = (vn * lng_ref[...] + lnb_ref[...]).astype(BF16)

    out = x
    for j, c0 in enumerate(blocks):
        u, z = ahead
        if j + 1 < len(blocks):
            ahead = project_uz(blocks[j + 1])
        u = _gelu_tanh(u)
        gate = _silu(z)
        for r0 in range(0, tm, CHUNK):
            for gc in range(0, GATE_COLS, GROUP_W):
                g = (c0 + gc) // GROUP_W
                cols = slice(c0 + gc, c0 + gc + GROUP_W)
                sv = jnp.dot(wsm_s[g], vn_s[r0:r0 + CHUNK, cols], preferred_element_type=F32)
                sv = sv + bsb_s[g]
                y = u[r0:r0 + CHUNK, gc:gc + GROUP_W] * sv * gate[r0:r0 + CHUNK, gc:gc + GROUP_W]
                y_s[r0:r0 + CHUNK, cols] = y.astype(BF16)
        out = out + jnp.dot(y_s[:, c0:c0 + GATE_COLS], wout_ref[c0:c0 + GATE_COLS, :],
                            preferred_element_type=F32)
    o_ref[...] = out


def _layer_a(x, norm_g, w_in, ln_g, ln_b, w_s, b_s, w_out):
    t, d = x.shape
    e = w_out.shape[0]
    n_groups = w_s.shape[0]
    tile = LAYER_A_TILE
    assert t % tile == 0 and tile % CHUNK == 0 and e % GATE_COLS == 0
    return pl.pallas_call(
        _layer_a_kernel,
        grid=(t // tile,),
        in_specs=[
            pl.BlockSpec((tile, d), lambda i: (i, 0)),
            _resident((1, d)),
            _resident((d, 3 * e)),
            _resident((1, e)),
            _resident((1, e)),
            _resident((n_groups, CHUNK, CHUNK)),
            _resident((n_groups, CHUNK, 1)),
            _resident((e, d)),
        ],
        out_specs=pl.BlockSpec((tile, d), lambda i: (i, 0)),
        out_shape=jax.ShapeDtypeStruct((t, d), F32),
        scratch_shapes=[
            pltpu.VMEM((n_groups, CHUNK, CHUNK), BF16),
            pltpu.VMEM((n_groups, CHUNK, GROUP_W), F32),
            pltpu.VMEM((tile, e), BF16),
            pltpu.VMEM((tile, e), BF16),
        ],
        compiler_params=pltpu.CompilerParams(
            dimension_semantics=("arbitrary",), vmem_limit_bytes=VMEM_LIMIT_BYTES),
        name="layer_a",
    )(x, norm_g.reshape(1, d), w_in.astype(BF16), ln_g.reshape(1, e), ln_b.reshape(1, e),
      w_s, b_s.reshape(n_groups, CHUNK, 1), w_out.astype(BF16))


def _proj_kernel(h_ref, kvg_ref, qg_ref, wk_ref, wq_ref, wvt_ref, k_ref, q_ref, vt_ref, hq_ref):
    h = h_ref[...]
    hr = h * lax.rsqrt(jnp.mean(h * h, axis=-1, keepdims=True) + EPS)
    hk = (hr * kvg_ref[...]).astype(BF16)
    hq = (hr * qg_ref[...]).astype(BF16)
    hq_ref[...] = hq
    k = jnp.dot(hk, wk_ref[...], preferred_element_type=F32).astype(BF16)
    q = jnp.dot(hq, wq_ref[...], preferred_element_type=F32)
    q = (q * QK_EXP2_SCALE).astype(BF16)
    hw = k_ref.shape[2]
    for head in range(k_ref.shape[0]):
        k_ref[head] = k[:, head * hw:(head + 1) * hw]
        q_ref[head] = q[:, head * hw:(head + 1) * hw]
    vt_ref[...] = lax.dot_general(wvt_ref[...], hk, _NT, preferred_element_type=F32).astype(BF16)


def _proj(h, kv_norm_g, q_norm_g, w_kv, w_q):
    t, d = h.shape
    w = w_kv.shape[1] // 2
    hw = 2 * DIFF_HEAD_DIM
    n_heads = w // hw
    tile = PROJ_TILE
    rows = pl.BlockSpec((n_heads, tile, hw), lambda i: (0, i, 0))
    cols = pl.BlockSpec((w, tile), lambda i: (0, i))
    return pl.pallas_call(
        _proj_kernel,
        grid=(t // tile,),
        in_specs=[
            pl.BlockSpec((tile, d), lambda i: (i, 0)),
            _resident((1, d)),
            _resident((1, d)),
            _resident((d, w)),
            _resident((d, w)),
            _resident((w, d)),
        ],
        out_specs=[rows, rows, cols, pl.BlockSpec((tile, d), lambda i: (i, 0))],
        out_shape=[jax.ShapeDtypeStruct((n_heads, t, hw), BF16), jax.ShapeDtypeStruct((n_heads, t, hw), BF16),
                   jax.ShapeDtypeStruct((w, t), BF16), jax.ShapeDtypeStruct((t, d), BF16)],
        compiler_params=pltpu.CompilerParams(
            dimension_semantics=("arbitrary",), vmem_limit_bytes=VMEM_LIMIT_BYTES),
        name="proj",
    )(h, kv_norm_g.reshape(1, d), q_norm_g.reshape(1, d),
      w_kv[:, :w].astype(BF16), w_q.astype(BF16), w_kv[:, w:].astype(BF16).T)


def _attn_kernel(lq1_ref, lk1_ref, lq2_ref, lk2_ref, sg_ref, q_ref, k_ref, vt_ref, hq_ref, wz_ref, yt_ref,
                 ex_s, wt_s, *, lam_init):
    seq = q_ref.shape[0]
    d = DIFF_HEAD_DIM
    lam = (jnp.exp(jnp.sum(lq1_ref[...] * lk1_ref[...], keepdims=True))
           - jnp.exp(jnp.sum(lq2_ref[...] * lk2_ref[...], keepdims=True)) + lam_init)
    gain = sg_ref[...] * (1.0 - lam_init)

    groups = KEY_CHUNK // SUBLANES

    grouped = (groups, SUBLANES, Q_TILE)

    def scores(r0):
        return [lax.dot_general(k_ref[0:r0 + Q_TILE, n * d:(n + 1) * d],
                                q_ref[r0:r0 + Q_TILE, n * d:(n + 1) * d],
                                _NT, preferred_element_type=F32) for n in range(2)]

    def weights(r0, st_pair):
        p_len = r0 + Q_TILE
        chunks = range(0, p_len, KEY_CHUNK)
        coef = []
        for n, st in enumerate(st_pair):
            m_c, l_c = [], []
            for c0 in chunks:
                s = st[c0:c0 + KEY_CHUNK, :]
                if c0 == r0:
                    key = lax.broadcasted_iota(jnp.int32, s.shape, 0)
                    query = lax.broadcasted_iota(jnp.int32, s.shape, 1)
                    s = jnp.where(key <= query, s, -jnp.inf)
                s = s.reshape(grouped)
                m = jnp.max(s, axis=0)
                shift = m
                if c0 == r0:
                    shift = jnp.where(m == -jnp.inf, 0.0, m)
                ex = jnp.exp2(s - shift)
                ex_s[n, c0:c0 + KEY_CHUNK, :] = ex.reshape(KEY_CHUNK, Q_TILE)
                m_c.append(m)
                l_c.append(jnp.sum(ex, axis=0))
            m_row = jnp.max(functools.reduce(jnp.maximum, m_c), axis=0, keepdims=True)
            m_row = jnp.broadcast_to(m_row, (SUBLANES, Q_TILE))
            alpha = [jnp.exp2(m - m_row) for m in m_c]
            l_row = jnp.sum(sum(a * l for a, l in zip(alpha, l_c)), axis=0, keepdims=True)
            inv = 1.0 / l_row
            if n == 1:
                inv = lam * inv
            inv = jnp.broadcast_to(inv, (SUBLANES, Q_TILE))
            coef.append([a * inv for a in alpha])
        for j, c0 in enumerate(chunks):
            w = (ex_s[0, c0:c0 + KEY_CHUNK, :].reshape(grouped) * coef[0][j]
                 - ex_s[1, c0:c0 + KEY_CHUNK, :].reshape(grouped) * coef[1][j])
            wt_s[c0:c0 + KEY_CHUNK, :] = w.reshape(KEY_CHUNK, Q_TILE).astype(BF16)

    def output(r0):
        p_len = r0 + Q_TILE
        ot = jnp.dot(vt_ref[:, 0:p_len], wt_s[0:p_len, :], preferred_element_type=F32)
        on = ot * lax.rsqrt(jnp.mean(ot * ot, axis=0, keepdims=True) + EPS)
        on = on * gain
        z = jnp.dot(hq_ref[r0:r0 + Q_TILE, :], wz_ref[...], preferred_element_type=F32).T
        yt_ref[:, r0:r0 + Q_TILE] = (on * _silu(z)).astype(BF16)

    tiles = list(range(0, seq, Q_TILE))
    tiles = tiles[::2] + tiles[1::2][::-1]
    ahead = scores(tiles[0])
    for i, r0 in enumerate(tiles):
        current = ahead
        if i + 1 < len(tiles):
            ahead = scores(tiles[i + 1])
        if i:
            output(tiles[i - 1])
        weights(r0, current)
    output(tiles[-1])


def _attn(q, k, vt, hq, w_z, lq1, lk1, lq2, lk2, subln_g, lam_init, batch, seq):
    n_heads, t, hw = q.shape
    w = n_heads * hw
    d_model = hq.shape[1]
    w_z = w_z.reshape(d_model, n_heads, hw).transpose(1, 0, 2).astype(BF16)
    assert seq % Q_TILE == 0 and KEY_CHUNK == Q_TILE and t == batch * seq
    rows = pl.BlockSpec((None, seq, hw), lambda b, h: (h, b, 0))
    cols = pl.BlockSpec((hw, seq), lambda b, h: (h, b))
    vec = _resident((1, DIFF_HEAD_DIM))
    return pl.pallas_call(
        functools.partial(_attn_kernel, lam_init=lam_init),
        grid=(batch, n_heads),
        in_specs=[vec, vec, vec, vec, _resident((hw, 1)), rows, rows, cols,
                  pl.BlockSpec((seq, d_model), lambda b, h: (b, 0)),
                  pl.BlockSpec((None, d_model, hw), lambda b, h: (h, 0, 0))],
        out_specs=cols,
        out_shape=jax.ShapeDtypeStruct((w, t), BF16),
        scratch_shapes=[
            pltpu.VMEM((2, seq, Q_TILE), F32),
            pltpu.VMEM((seq, Q_TILE), BF16),
        ],
        compiler_params=pltpu.CompilerParams(
            dimension_semantics=("arbitrary", "arbitrary"), vmem_limit_bytes=VMEM_LIMIT_BYTES),
        name="attn",
    )(lq1.reshape(1, -1), lk1.reshape(1, -1), lq2.reshape(1, -1), lk2.reshape(1, -1),
      subln_g.reshape(hw, 1), q, k, vt, hq, w_z)


def _out_proj_kernel(h_ref, yt_ref, wot_ref, fg_ref, o_ref):
    tn = (((0,), (0,)), ((), ()))
    h = h_ref[...] + lax.dot_general(yt_ref[...], wot_ref[...], tn, preferred_element_type=F32)
    hn = h * lax.rsqrt(jnp.mean(h * h, axis=-1, keepdims=True) + EPS)
    o_ref[...] = hn * fg_ref[...]


def _out_proj(h, yt, w_o, final_g):
    t, d = h.shape
    w = yt.shape[0]
    tile = OUT_PROJ_TILE
    return pl.pallas_call(
        _out_proj_kernel,
        grid=(t // tile,),
        in_specs=[
            pl.BlockSpec((tile, d), lambda i: (i, 0)),
            pl.BlockSpec((w, tile), lambda i: (0, i)),
            _resident((w, d)),
            _resident((1, d)),
        ],
        out_specs=pl.BlockSpec((tile, d), lambda i: (i, 0)),
        out_shape=jax.ShapeDtypeStruct((t, d), F32),
        compiler_params=pltpu.CompilerParams(
            dimension_semantics=("arbitrary",), vmem_limit_bytes=VMEM_LIMIT_BYTES),
        name="out_proj",
    )(h, yt, w_o.astype(BF16), final_g.reshape(1, d))


def _lambda_init(layer_idx):
    return 0.8 - 0.6 * math.exp(-0.3 * layer_idx)


def kernel(x, a_norm_g, a_w_in, a_ln_g, a_ln_b, a_w_s, a_b_s, a_w_out,
           b_norm_g, b_w_qz, b_lam_q1, b_lam_k1, b_lam_q2, b_lam_k2, b_subln_g, b_w_o,
           kv_norm_g, w_kv, final_g):
    batch, seq, d = x.shape
    n_a, n_b = a_norm_g.shape[0], b_norm_g.shape[0]
    assert n_a == 1 and n_b == 1, "one gMLP layer followed by one attention layer"
    h = x.reshape(batch * seq, d)
    h = _layer_a(h, a_norm_g[0], a_w_in[0], a_ln_g[0], a_ln_b[0], a_w_s[0], a_b_s[0], a_w_out[0])
    att_w = w_kv.shape[1] // 2
    k, q, vt, hq = _proj(h, kv_norm_g, b_norm_g[0], w_kv, b_w_qz[0][:, :att_w])
    yt = _attn(q, k, vt, hq, b_w_qz[0][:, att_w:], b_lam_q1[0], b_lam_k1[0], b_lam_q2[0], b_lam_k2[0], b_subln_g[0],
               _lambda_init(n_a), batch, seq)
    out = _out_proj(h, yt, b_w_o[0], final_g)
    return out.reshape(batch, seq, d)
```

```python
import functools
import math

import jax
import jax.numpy as jnp
from jax import lax
from jax.experimental import pallas as pl
from jax.experimental.pallas import tpu as pltpu

F32 = jnp.float32
BF16 = jnp.bfloat16

CHUNK = 128
GROUP_W = 128
DIFF_HEAD_DIM = 128
EPS = 1e-6
SUBLANES = 8
QK_EXP2_SCALE = DIFF_HEAD_DIM ** -0.5 * math.log2(math.e)

LAYER_A_TILE = 512
PROJ_TILE = 1024
OUT_PROJ_TILE = 1024
GATE_COLS = 1024
Q_TILE = 256
KEY_CHUNK = Q_TILE
VMEM_LIMIT_BYTES = 60 * 1024 * 1024

_NT = (((1,), (1,)), ((), ()))


def _silu(x):
    h = 0.5 * x
    return h + h * jnp.tanh(h)


def _gelu_tanh(x):
    c = math.sqrt(2.0 / math.pi)
    h = 0.5 * x
    return h + h * jnp.tanh(x * (c + (c * 0.044715) * (x * x)))


def _resident(shape):
    return pl.BlockSpec(shape, lambda *_: (0,) * len(shape), pipeline_mode=pl.Buffered(1))


def _layer_a_kernel(x_ref, ng_ref, win_ref, lng_ref, lnb_ref, ws_ref, bs_ref, wout_ref,
                    o_ref, wsm_s, bsb_s, vn_s, y_s):
    tm, _ = x_ref.shape
    e = vn_s.shape[1]
    n_groups = ws_ref.shape[0]

    @pl.when(pl.program_id(0) == 0)
    def _():
        row = lax.broadcasted_iota(jnp.int32, (CHUNK, CHUNK), 0)
        col = lax.broadcasted_iota(jnp.int32, (CHUNK, CHUNK), 1)
        causal = col <= row
        for g in range(n_groups):
            wsm_s[g] = jnp.where(causal, ws_ref[g], 0.0).astype(BF16)
            bsb_s[g] = jnp.broadcast_to(bs_ref[g], (CHUNK, GROUP_W))

    x = x_ref[...]
    hn = x * lax.rsqrt(jnp.mean(x * x, axis=-1, keepdims=True) + EPS)
    hn = (hn * ng_ref[...]).astype(BF16)

    def project_uz(c0):
        return (jnp.dot(hn, win_ref[:, c0:c0 + GATE_COLS], preferred_element_type=F32),
                jnp.dot(hn, win_ref[:, 2 * e + c0:2 * e + c0 + GATE_COLS], preferred_element_type=F32))

    v = jnp.dot(hn, win_ref[:, e:2 * e], preferred_element_type=F32)
    blocks = list(range(0, e, GATE_COLS))
    ahead = project_uz(blocks[0])

    v = _gelu_tanh(v)
    mu = jnp.mean(v, axis=-1, keepdims=True)
    vc = v - mu
    vn = vc * lax.rsqrt(jnp.mean(vc * vc, axis=-1, keepdims=True) + EPS)
    vn_s[...] = (vn * lng_ref[...] + lnb_ref[...]).astype(BF16)

    out = x
    for j, c0 in enumerate(blocks):
        u, z = ahead
        if j + 1 < len(blocks):
            ahead = project_uz(blocks[j + 1])
        u = _gelu_tanh(u)
        gate = _silu(z)
        for r0 in range(0, tm, CHUNK):
            for gc in range(0, GATE_COLS, GROUP_W):
                g = (c0 + gc) // GROUP_W
                cols = slice(c0 + gc, c0 + gc + GROUP_W)
                sv = jnp.dot(wsm_s[g], vn_s[r0:r0 + CHUNK, cols], preferred_element_type=F32)
                sv = sv + bsb_s[g]
                y = u[r0:r0 + CHUNK, gc:gc + GROUP_W] * sv * gate[r0:r0 + CHUNK, gc:gc + GROUP_W]
                y_s[r0:r0 + CHUNK, cols] = y.astype(BF16)
        out = out + jnp.dot(y_s[:, c0:c0 + GATE_COLS], wout_ref[c0:c0 + GATE_COLS, :],
                            preferred_element_type=F32)
    o_ref[...] = out


def _layer_a(x, norm_g, w_in, ln_g, ln_b, w_s, b_s, w_out):
    t, d = x.shape
    e = w_out.shape[0]
    n_groups = w_s.shape[0]
    tile = LAYER_A_TILE
    assert t % tile == 0 and tile % CHUNK == 0 and e % GATE_COLS == 0
    return pl.pallas_call(
        _layer_a_kernel,
        grid=(t // tile,),
        in_specs=[
            pl.BlockSpec((tile, d), lambda i: (i, 0)),
            _resident((1, d)),
            _resident((d, 3 * e)),
            _resident((1, e)),
            _resident((1, e)),
            _resident((n_groups, CHUNK, CHUNK)),
            _resident((n_groups, CHUNK, 1)),
            _resident((e, d)),
        ],
        out_specs=pl.BlockSpec((tile, d), lambda i: (i, 0)),
        out_shape=jax.ShapeDtypeStruct((t, d), F32),
        scratch_shapes=[
            pltpu.VMEM((n_groups, CHUNK, CHUNK), BF16),
            pltpu.VMEM((n_groups, CHUNK, GROUP_W), F32),
            pltpu.VMEM((tile, e), BF16),
            pltpu.VMEM((tile, e), BF16),
        ],
        compiler_params=pltpu.CompilerParams(
            dimension_semantics=("arbitrary",), vmem_limit_bytes=VMEM_LIMIT_BYTES),
        name="layer_a",
    )(x, norm_g.reshape(1, d), w_in.astype(BF16), ln_g.reshape(1, e), ln_b.reshape(1, e),
      w_s, b_s.reshape(n_groups, CHUNK, 1), w_out.astype(BF16))


def _proj_kernel(h_ref, kvg_ref, qg_ref, wk_ref, wq_ref, wv_ref, k_ref, q_ref, v_ref, hq_ref):
    h = h_ref[...]
    hr = h * lax.rsqrt(jnp.mean(h * h, axis=-1, keepdims=True) + EPS)
    hk = (hr * kvg_ref[...]).astype(BF16)
    hq = (hr * qg_ref[...]).astype(BF16)
    hq_ref[...] = hq
    k = jnp.dot(hk, wk_ref[...], preferred_element_type=F32).astype(BF16)
    q = jnp.dot(hq, wq_ref[...], preferred_element_type=F32)
    q = (q * QK_EXP2_SCALE).astype(BF16)
    hw = k_ref.shape[2]
    for head in range(k_ref.shape[0]):
        k_ref[head] = k[:, head * hw:(head + 1) * hw]
        q_ref[head] = q[:, head * hw:(head + 1) * hw]
    v = jnp.dot(hk, wv_ref[...], preferred_element_type=F32).astype(BF16)
    for head in range(k_ref.shape[0]):
        v_ref[head] = v[:, head * hw:(head + 1) * hw]


def _proj(h, kv_norm_g, q_norm_g, w_kv, w_q):
    t, d = h.shape
    w = w_kv.shape[1] // 2
    hw = 2 * DIFF_HEAD_DIM
    n_heads = w // hw
    tile = PROJ_TILE
    rows = pl.BlockSpec((n_heads, tile, hw), lambda i: (0, i, 0))
    cols = pl.BlockSpec((w, tile), lambda i: (0, i))
    return pl.pallas_call(
        _proj_kernel,
        grid=(t // tile,),
        in_specs=[
            pl.BlockSpec((tile, d), lambda i: (i, 0)),
            _resident((1, d)),
            _resident((1, d)),
            _resident((d, w)),
            _resident((d, w)),
            _resident((d, w)),
        ],
        out_specs=[rows, rows, rows, pl.BlockSpec((tile, d), lambda i: (i, 0))],
        out_shape=[jax.ShapeDtypeStruct((n_heads, t, hw), BF16)] * 3 + [jax.ShapeDtypeStruct((t, d), BF16)],
        compiler_params=pltpu.CompilerParams(
            dimension_semantics=("arbitrary",), vmem_limit_bytes=VMEM_LIMIT_BYTES),
        name="proj",
    )(h, kv_norm_g.reshape(1, d), q_norm_g.reshape(1, d),
      w_kv[:, :w].astype(BF16), w_q.astype(BF16), w_kv[:, w:].astype(BF16))


def _attn_kernel(lq1_ref, lk1_ref, lq2_ref, lk2_ref, sg_ref, q_ref, k_ref, v_ref, hq_ref, wz_ref, yt_ref,
                 ex_s, wt_s, *, lam_init):
    seq = q_ref.shape[0]
    d = DIFF_HEAD_DIM
    lam = (jnp.exp(jnp.sum(lq1_ref[...] * lk1_ref[...], keepdims=True))
           - jnp.exp(jnp.sum(lq2_ref[...] * lk2_ref[...], keepdims=True)) + lam_init)
    gain = sg_ref[...] * (1.0 - lam_init)

    groups = KEY_CHUNK // SUBLANES

    grouped = (groups, SUBLANES, Q_TILE)

    def scores(r0):
        return [lax.dot_general(k_ref[0:r0 + Q_TILE, n * d:(n + 1) * d],
                                q_ref[r0:r0 + Q_TILE, n * d:(n + 1) * d],
                                _NT, preferred_element_type=F32) for n in range(2)]

    def weights(r0, st_pair):
        p_len = r0 + Q_TILE
        chunks = range(0, p_len, KEY_CHUNK)
        coef = []
        for n, st in enumerate(st_pair):
            m_c, l_c = [], []
            for c0 in chunks:
                s = st[c0:c0 + KEY_CHUNK, :]
                if c0 == r0:
                    key = lax.broadcasted_iota(jnp.int32, s.shape, 0)
                    query = lax.broadcasted_iota(jnp.int32, s.shape, 1)
                    s = jnp.where(key <= query, s, -jnp.inf)
                s = s.reshape(grouped)
                m = jnp.max(s, axis=0)
                shift = m
                if c0 == r0:
                    shift = jnp.where(m == -jnp.inf, 0.0, m)
                ex = jnp.exp2(s - shift)
                ex_s[n, c0:c0 + KEY_CHUNK, :] = ex.reshape(KEY_CHUNK, Q_TILE)
                m_c.append(m)
                l_c.append(jnp.sum(ex, axis=0))
            m_row = jnp.max(functools.reduce(jnp.maximum, m_c), axis=0, keepdims=True)
            m_row = jnp.broadcast_to(m_row, (SUBLANES, Q_TILE))
            alpha = [jnp.exp2(m - m_row) for m in m_c]
            l_row = jnp.sum(sum(a * l for a, l in zip(alpha, l_c)), axis=0, keepdims=True)
            inv = 1.0 / l_row
            if n == 1:
                inv = lam * inv
            inv = jnp.broadcast_to(inv, (SUBLANES, Q_TILE))
            coef.append([a * inv for a in alpha])
        for j, c0 in enumerate(chunks):
            w = (ex_s[0, c0:c0 + KEY_CHUNK, :].reshape(grouped) * coef[0][j]
                 - ex_s[1, c0:c0 + KEY_CHUNK, :].reshape(grouped) * coef[1][j])
            wt_s[c0:c0 + KEY_CHUNK, :] = w.reshape(KEY_CHUNK, Q_TILE).astype(BF16)

    def output(r0):
        p_len = r0 + Q_TILE
        ot = lax.dot_general(v_ref[0:p_len, :], wt_s[0:p_len, :], (((0,), (0,)), ((), ())),
                             preferred_element_type=F32)
        on = ot * lax.rsqrt(jnp.mean(ot * ot, axis=0, keepdims=True) + EPS)
        on = on * gain
        z = jnp.dot(hq_ref[r0:r0 + Q_TILE, :], wz_ref[...], preferred_element_type=F32).T
        yt_ref[:, r0:r0 + Q_TILE] = (on * _silu(z)).astype(BF16)

    tiles = list(range(0, seq, Q_TILE))
    tiles = tiles[::2] + tiles[1::2][::-1]
    ahead = scores(tiles[0])
    for i, r0 in enumerate(tiles):
        current = ahead
        if i + 1 < len(tiles):
            ahead = scores(tiles[i + 1])
        if i:
            output(tiles[i - 1])
        weights(r0, current)
    output(tiles[-1])


def _attn(q, k, vt, hq, w_z, lq1, lk1, lq2, lk2, subln_g, lam_init, batch, seq):
    n_heads, t, hw = q.shape
    w = n_heads * hw
    d_model = hq.shape[1]
    w_z = w_z.reshape(d_model, n_heads, hw).transpose(1, 0, 2).astype(BF16)
    assert seq % Q_TILE == 0 and KEY_CHUNK == Q_TILE and t == batch * seq
    rows = pl.BlockSpec((None, seq, hw), lambda b, h: (h, b, 0))
    cols = pl.BlockSpec((hw, seq), lambda b, h: (h, b))
    vec = _resident((1, DIFF_HEAD_DIM))
    return pl.pallas_call(
        functools.partial(_attn_kernel, lam_init=lam_init),
        grid=(batch, n_heads),
        in_specs=[vec, vec, vec, vec, _resident((hw, 1)), rows, rows, rows,
                  pl.BlockSpec((seq, d_model), lambda b, h: (b, 0)),
                  pl.BlockSpec((None, d_model, hw), lambda b, h: (h, 0, 0))],
        out_specs=cols,
        out_shape=jax.ShapeDtypeStruct((w, t), BF16),
        scratch_shapes=[
            pltpu.VMEM((2, seq, Q_TILE), F32),
            pltpu.VMEM((seq, Q_TILE), BF16),
        ],
        compiler_params=pltpu.CompilerParams(
            dimension_semantics=("arbitrary", "arbitrary"), vmem_limit_bytes=VMEM_LIMIT_BYTES),
        name="attn",
    )(lq1.reshape(1, -1), lk1.reshape(1, -1), lq2.reshape(1, -1), lk2.reshape(1, -1),
      subln_g.reshape(hw, 1), q, k, vt, hq, w_z)


def _out_proj_kernel(h_ref, yt_ref, wot_ref, fg_ref, o_ref):
    tn = (((0,), (0,)), ((), ()))
    h = h_ref[...] + lax.dot_general(yt_ref[...], wot_ref[...], tn, preferred_element_type=F32)
    hn = h * lax.rsqrt(jnp.mean(h * h, axis=-1, keepdims=True) + EPS)
    o_ref[...] = hn * fg_ref[...]


def _out_proj(h, yt, w_o, final_g):
    t, d = h.shape
    w = yt.shape[0]
    tile = OUT_PROJ_TILE
    return pl.pallas_call(
        _out_proj_kernel,
        grid=(t // tile,),
        in_specs=[
            pl.BlockSpec((tile, d), lambda i: (i, 0)),
            pl.BlockSpec((w, tile), lambda i: (0, i)),
            _resident((w, d)),
            _resident((1, d)),
        ],
        out_specs=pl.BlockSpec((tile, d), lambda i: (i, 0)),
        out_shape=jax.ShapeDtypeStruct((t, d), F32),
        compiler_params=pltpu.CompilerParams(
            dimension_semantics=("arbitrary",), vmem_limit_bytes=VMEM_LIMIT_BYTES),
        name="out_proj",
    )(h, yt, w_o.astype(BF16), final_g.reshape(1, d))


def _lambda_init(layer_idx):
    return 0.8 - 0.6 * math.exp(-0.3 * layer_idx)


def kernel(x, a_norm_g, a_w_in, a_ln_g, a_ln_b, a_w_s, a_b_s, a_w_out,
           b_norm_g, b_w_qz, b_lam_q1, b_lam_k1, b_lam_q2, b_lam_k2, b_subln_g, b_w_o,
           kv_norm_g, w_kv, final_g):
    batch, seq, d = x.shape
    n_a, n_b = a_norm_g.shape[0], b_norm_g.shape[0]
    assert n_a == 1 and n_b == 1, "one gMLP layer followed by one attention layer"
    h = x.reshape(batch * seq, d)
    h = _layer_a(h, a_norm_g[0], a_w_in[0], a_ln_g[0], a_ln_b[0], a_w_s[0], a_b_s[0], a_w_out[0])
    att_w = w_kv.shape[1] // 2
    k, q, vt, hq = _proj(h, kv_norm_g, b_norm_g[0], w_kv, b_w_qz[0][:, :att_w])
    yt = _attn(q, k, vt, hq, b_w_qz[0][:, att_w:], b_lam_q1[0], b_lam_k1[0], b_lam_q2[0], b_lam_k2[0], b_subln_g[0],
               _lambda_init(n_a), batch, seq)
    out = _out_proj(h, yt, b_w_o[0], final_g)
    return out.reshape(batch, seq, d)
```
